```python
import jax, jax.numpy as jnp
from jax import lax
import numpy as np

D_MODEL = 2048
BATCH = 2
SEQ = 4096
DEPTH = 1
DEC_BATCH = 32
DEC_SEQ = 8
PAST_LEN = 8192
PAGE_SIZE = 128

HEAD_DIM = 128
N_HEADS = D_MODEL // HEAD_DIM
H_A = N_HEADS // 2
H_B = N_HEADS - H_A
KV_B = 2
GROUP_B = H_B // KV_B
H_IDX = 16
D_IDX = 64
TOPK_MAX = 256
Q_BLOCK = 128
ROPE_THETA = 10000.0
EPS = 1e-6
FORGET_BIAS_MEAN = 2.0
D_FF = 128 * ((8 * D_MODEL // 3 + 127) // 128)
CONV_W = 3
IN_SPLITS = (H_A * HEAD_DIM, H_A * HEAD_DIM, H_A * HEAD_DIM, H_A,
             H_B * HEAD_DIM, KV_B * HEAD_DIM, KV_B * HEAD_DIM,
             H_IDX * D_IDX, D_IDX, H_IDX)
D_IN = sum(IN_SPLITS)

kernel_name = 'hymba_fox_dsa_convffn_step'


def rms_norm(x, g):
    xf = x.astype(jnp.float32)
    y = xf * lax.rsqrt(jnp.mean(xf * xf, axis=-1, keepdims=True) + EPS)
    return (y * g.astype(jnp.float32)).astype(x.dtype)


def rope(x, pos):
    half = x.shape[-1] // 2
    inv = ROPE_THETA ** (-jnp.arange(half, dtype=jnp.float32) / half)
    ang = pos.astype(jnp.float32)[:, None] * inv[None, :]
    cos = jnp.cos(ang)[:, None, :]
    sin = jnp.sin(ang)[:, None, :]
    xf = x.astype(jnp.float32)
    x1, x2 = xf[..., :half], xf[..., half:]
    return jnp.concatenate([x1 * cos - x2 * sin, x2 * cos + x1 * sin], axis=-1).astype(x.dtype)


def project(xn, pos, w_in, b_f, g_qa, g_ka, g_qb, g_kb):
    B, T, _ = xn.shape
    points = [int(v) for v in np.cumsum(IN_SPLITS)[:-1]]
    qa, ka, va, fa, qb, kb, vb, qi, ki, wi = jnp.split(xn @ w_in, points, axis=-1)
    qa = rms_norm(qa.reshape(B, T, H_A, HEAD_DIM), g_qa)
    ka = rms_norm(ka.reshape(B, T, H_A, HEAD_DIM), g_ka)
    va = va.reshape(B, T, H_A, HEAD_DIM)
    logf = jax.nn.log_sigmoid((fa + b_f).astype(jnp.float32)).astype(xn.dtype)
    qb = rope(rms_norm(qb.reshape(B, T, H_B, HEAD_DIM), g_qb), pos)
    kb = rope(rms_norm(kb.reshape(B, T, KV_B, HEAD_DIM), g_kb), pos)
    vb = vb.reshape(B, T, KV_B, HEAD_DIM)
    qi = rope(qi.reshape(B, T, H_IDX, D_IDX), pos)
    ki = rope(ki[:, :, None, :], pos)[:, :, 0, :]
    return qa, ka, va, logf, qb, kb, vb, qi, ki, wi


def _blocks(n):
    qb = Q_BLOCK if n % Q_BLOCK == 0 else n
    return qb, n // qb


def _to_blocks(a, nb, qb):
    return jnp.swapaxes(a.reshape(a.shape[0], nb, qb, *a.shape[2:]), 0, 1)


def _from_blocks(a):
    a = jnp.swapaxes(a, 0, 1)
    return a.reshape(a.shape[0], a.shape[1] * a.shape[2], *a.shape[3:])


def forgetting_attention(q, k, v, logf, q_pos):
    Tk = k.shape[1]
    qb, nb = _blocks(q.shape[1])
    c = jnp.cumsum(logf.astype(jnp.float32), axis=1)
    c_q = jnp.take(c, q_pos, axis=1)
    c_k = jnp.swapaxes(c, 1, 2)
    k_pos = jnp.arange(Tk)
    scale = HEAD_DIM ** -0.5

    def block(args):
        qblk, cqblk, pblk = args
        s = jnp.einsum('bqhd,bkhd->bhqk', qblk, k, preferred_element_type=jnp.float32) * scale
        s = s + jnp.swapaxes(cqblk, 1, 2)[..., None] - c_k[:, :, None, :]
        s = jnp.where((k_pos[None, :] <= pblk[:, None])[None, None], s, -jnp.inf)
        p = jax.nn.softmax(s, axis=-1).astype(v.dtype)
        return jnp.einsum('bhqk,bkhd->bqhd', p, v)

    out = lax.map(block, (_to_blocks(q, nb, qb), _to_blocks(c_q, nb, qb), q_pos.reshape(nb, qb)))
    return _from_blocks(out)


def indexed_sparse_attention(q, k, v, qi, ki, wi, q_pos):
    B, Tk = k.shape[:2]
    n_sel = min(TOPK_MAX, Tk // 4)
    qb, nb = _blocks(q.shape[1])
    k_pos = jnp.arange(Tk)
    scale = HEAD_DIM ** -0.5
    idx_scale = (H_IDX * D_IDX) ** -0.5
    gather = jax.vmap(lambda rows, ids: rows[ids])

    def block(args):
        qblk, qiblk, wiblk, pblk = args
        causal = k_pos[None, :] <= pblk[:, None]
        dots = jnp.einsum('bqhd,bkd->bqhk', qiblk, ki, preferred_element_type=jnp.float32)
        score = jnp.einsum('bqhk,bqh->bqk', jax.nn.relu(dots), wiblk.astype(jnp.float32)) * idx_scale
        score = jnp.where(causal[None], score, -jnp.inf)
        _, sel = lax.top_k(score, n_sel)
        valid = sel <= pblk[None, :, None]
        ks = gather(k, sel)
        vs = gather(v, sel)
        qg = qblk.reshape(B, qb, KV_B, GROUP_B, HEAD_DIM)
        s = jnp.einsum('bqgrd,bqngd->bqgrn', qg, ks, preferred_element_type=jnp.float32) * scale
        s = jnp.where(valid[:, :, None, None, :], s, -jnp.inf)
        p = jax.nn.softmax(s, axis=-1).astype(vs.dtype)
        o = jnp.einsum('bqgrn,bqngd->bqgrd', p, vs)
        return o.reshape(B, qb, H_B, HEAD_DIM)

    out = lax.map(block, (_to_blocks(q, nb, qb), _to_blocks(qi, nb, qb),
                          _to_blocks(wi, nb, qb), q_pos.reshape(nb, qb)))
    return _from_blocks(out)


def conv_ffn(hn, prev_g, w_gate, w_up, conv_w, conv_b, w_down):
    T = hn.shape[1]
    g = hn @ w_gate
    u = hn @ w_up
    gp = jnp.concatenate([prev_g.astype(g.dtype), g], axis=1)
    gc = conv_b + sum(conv_w[j] * gp[:, j:j + T] for j in range(CONV_W))
    return (jax.nn.silu(gc) * u) @ w_down, gp[:, T:]


def decoder_layer(x, q_pos, past, prev_g, w_in, b_f, g_qa, g_ka, g_qb, g_kb, g_attn, w_out,
                  g_ffn, w_gate, w_up, conv_w, conv_b, w_down):
    B, T, _ = x.shape
    xn = rms_norm(x, g_attn)
    qa, ka, va, logf, qb, kb, vb, qi, ki, wi = project(xn, q_pos, w_in, b_f, g_qa, g_ka, g_qb, g_kb)
    new_rows = (ka, va, logf, kb, vb, ki)
    if past is None:
        ka_all, va_all, logf_all, kb_all, vb_all, ki_all = new_rows
    else:
        ka_all, va_all, logf_all, kb_all, vb_all, ki_all = [
            jnp.concatenate([p_, n_.astype(p_.dtype)], axis=1) for p_, n_ in zip(past, new_rows)]
    oa = forgetting_attention(qa, ka_all, va_all, logf_all, q_pos)
    ob = indexed_sparse_attention(qb, kb_all, vb_all, qi, ki_all, wi, q_pos)
    h = x + jnp.concatenate([oa.reshape(B, T, -1), ob.reshape(B, T, -1)], axis=-1) @ w_out
    f, new_g = conv_ffn(rms_norm(h, g_ffn), prev_g, w_gate, w_up, conv_w, conv_b, w_down)
    return h + f, new_rows + (new_g,)


def gather_pages(pool, layer, page_table):
    rows = pool[layer, page_table]
    return rows.reshape(rows.shape[0], rows.shape[1] * rows.shape[2], *rows.shape[3:])


def setup_inputs(seed: int = 0) -> dict:
    key = jax.random.key(seed)
    k = jax.random.split(key, 24)
    n_pages = PAST_LEN // PAGE_SIZE
    n_used = DEC_BATCH * n_pages
    n_pool = n_used + (n_used + 3) // 4

    def normal(kk, shape, scale=1.0):
        return scale * jax.random.normal(kk, shape, jnp.float32)

    pool = (DEPTH, n_pool, PAGE_SIZE)
    return {
        'x_prompt': normal(k[0], (BATCH, SEQ, D_MODEL)),
        'x_sample': normal(k[1], (DEC_BATCH, DEC_SEQ, D_MODEL)),
        'cache_fox_k': normal(k[2], pool + (H_A, HEAD_DIM)),
        'cache_fox_v': normal(k[3], pool + (H_A, HEAD_DIM)),
        'cache_fox_logf': jax.nn.log_sigmoid(FORGET_BIAS_MEAN + normal(k[4], pool + (H_A,))),
        'cache_dsa_k': normal(k[5], pool + (KV_B, HEAD_DIM)),
        'cache_dsa_v': normal(k[6], pool + (KV_B, HEAD_DIM)),
        'cache_idx_k': normal(k[7], pool + (D_IDX,)),
        'state_ffn_conv': normal(k[8], (DEPTH, DEC_BATCH, CONV_W - 1, D_FF)),
        'page_table': jax.random.permutation(k[9], n_pool)[:n_used].reshape(DEC_BATCH, n_pages).astype(jnp.int32),
        'w_in': normal(k[10], (DEPTH, D_MODEL, D_IN), D_MODEL ** -0.5),
        'b_f': FORGET_BIAS_MEAN + normal(k[11], (DEPTH, H_A), 0.1),
        'g_qa': 1.0 + normal(k[12], (DEPTH, HEAD_DIM), 0.02),
        'g_ka': 1.0 + normal(k[13], (DEPTH, HEAD_DIM), 0.02),
        'g_qb': 1.0 + normal(k[14], (DEPTH, HEAD_DIM), 0.02),
        'g_kb': 1.0 + normal(k[15], (DEPTH, HEAD_DIM), 0.02),
        'g_attn': 1.0 + normal(k[16], (DEPTH, D_MODEL), 0.02),
        'w_out': normal(k[17], (DEPTH, D_MODEL, D_MODEL), D_MODEL ** -0.5),
        'g_ffn': 1.0 + normal(k[18], (DEPTH, D_MODEL), 0.02),
        'w_gate': normal(k[19], (DEPTH, D_MODEL, D_FF), D_MODEL ** -0.5),
        'w_up': normal(k[20], (DEPTH, D_MODEL, D_FF), D_MODEL ** -0.5),
        'conv_w': normal(k[21], (DEPTH, CONV_W, D_FF), CONV_W ** -0.5),
        'conv_b': normal(k[22], (DEPTH, D_FF), 0.02),
        'w_down': normal(k[23], (DEPTH, D_FF, D_MODEL), D_FF ** -0.5),
    }


def reference(x_prompt, x_sample, cache_fox_k, cache_fox_v, cache_fox_logf, cache_dsa_k, cache_dsa_v,
              cache_idx_k, state_ffn_conv, page_table, w_in, b_f, g_qa, g_ka, g_qb, g_kb, g_attn,
              w_out, g_ffn, w_gate, w_up, conv_w, conv_b, w_down):
    past_len = page_table.shape[1] * PAGE_SIZE
    pos_p = jnp.arange(x_prompt.shape[1], dtype=jnp.int32)
    pos_s = past_len + jnp.arange(x_sample.shape[1], dtype=jnp.int32)
    pools = (cache_fox_k, cache_fox_v, cache_fox_logf, cache_dsa_k, cache_dsa_v, cache_idx_k)
    y_p, y_s = x_prompt, x_sample
    rows_p, rows_s = [], []
    for l in range(DEPTH):
        lw = (w_in[l], b_f[l], g_qa[l], g_ka[l], g_qb[l], g_kb[l], g_attn[l], w_out[l], g_ffn[l],
              w_gate[l], w_up[l], conv_w[l], conv_b[l], w_down[l])
        zero_g = jnp.zeros((y_p.shape[0], CONV_W - 1, D_FF), y_p.dtype)
        y_p, new_p = decoder_layer(y_p, pos_p, None, zero_g, *lw)
        past = tuple(gather_pages(pl, l, page_table) for pl in pools)
        y_s, new_s = decoder_layer(y_s, pos_s, past, state_ffn_conv[l], *lw)
        rows_p.append(new_p)
        rows_s.append(new_s)
    fox_k_p, fox_v_p, fox_logf_p, dsa_k_p, dsa_v_p, idx_k_p, conv_p = [jnp.stack(t) for t in zip(*rows_p)]
    fox_k_s, fox_v_s, fox_logf_s, dsa_k_s, dsa_v_s, idx_k_s, conv_s = [jnp.stack(t) for t in zip(*rows_s)]
    return (y_p, y_s, fox_k_p, fox_v_p, fox_logf_p, dsa_k_p, dsa_v_p, idx_k_p, conv_p,
            fox_k_s, fox_v_s, fox_logf_s, dsa_k_s, dsa_v_s, idx_k_s, conv_s)
```

```python
import functools

import jax
import jax.numpy as jnp
import numpy as np
from jax import lax
from jax.experimental import pallas as pl
from jax.experimental.pallas import tpu as pltpu

F32 = jnp.float32
BF16 = jnp.bfloat16
I32 = jnp.int32

LANES = 128
SUBLANES = 8
VMEM_CAP_BYTES = 56 * 1024 * 1024

HEAD_DIM = 128
H_A = 8
H_B = 8
KV_B = 2
GROUP_B = H_B // KV_B
H_IDX = 16
D_IDX = 64
TOPK_MAX = 256
PAGE_SIZE = 128
ROPE_THETA = 10000.0
EPS = 1e-6
CONV_W = 3

NEG = -1e30
INT_MIN = -(2 ** 31)
INT_MAX = 2 ** 31 - 1
NEG_INF_KEY = int(np.int32(np.uint32(0xFF800000) ^ np.uint32(0x7FFFFFFF)))

NT_DIMS = (((1,), (1,)), ((), ()))


def _nt_dot(a, b):
    return lax.dot_general(a, b, NT_DIMS, preferred_element_type=F32)


def _dot(a, b):
    return jnp.dot(a, b, preferred_element_type=F32)


def _params(semantics, vmem_bytes):
    limit = int(min(max(vmem_bytes, 16 * 1024 * 1024), VMEM_CAP_BYTES))
    return pltpu.CompilerParams(dimension_semantics=semantics, vmem_limit_bytes=limit)


def _nbytes(shape, dtype):
    return int(np.prod(shape)) * jnp.dtype(dtype).itemsize


def _f32_to_key(x):
    b = lax.bitcast_convert_type(x, I32)
    return b ^ ((b >> 31) & INT_MAX)


def _key_to_f32(k):
    b = k ^ ((k >> 31) & INT_MAX)
    return lax.bitcast_convert_type(b, F32)


def _rms_kernel(x_ref, g_ref, o_ref):
    x = x_ref[...]
    y = x * lax.rsqrt(jnp.mean(x * x, axis=-1, keepdims=True) + EPS)
    o_ref[...] = (y * g_ref[...]).astype(o_ref.dtype)


def _rms_call(x, g, tm):
    n, d = x.shape
    return pl.pallas_call(
        _rms_kernel,
        out_shape=jax.ShapeDtypeStruct((n, d), BF16),
        grid=(n // tm,),
        in_specs=[pl.BlockSpec((tm, d), lambda i: (i, 0)),
                  pl.BlockSpec((1, d), lambda i: (0, 0))],
        out_specs=pl.BlockSpec((tm, d), lambda i: (i, 0)),
        compiler_params=_params(("parallel",), 2 * (_nbytes((tm, d), F32) + _nbytes((tm, d), BF16)) + (4 << 20)),
        name="rms_norm",
    )(x, g.reshape(1, d))


def _rope_lanes(y, cos, sin, dim):
    if dim == LANES:
        rot = pltpu.roll(y, LANES // 2, 1)
    else:
        lane = lax.broadcasted_iota(I32, y.shape, 1)
        first_half = (lane % dim) < (dim // 2)
        rot = jnp.where(first_half, pltpu.roll(y, LANES - dim // 2, 1), pltpu.roll(y, dim // 2, 1))
    return y * cos + rot * sin


def _proj_kernel(*refs, has_gain, rope_dim, n32, n16):
    refs = list(refs)
    xn_ref, w_ref = refs[0], refs[1]
    pos = 2
    gain_ref = cos_ref = sin_ref = None
    if has_gain:
        gain_ref = refs[pos]
        pos += 1
    if rope_dim:
        cos_ref, sin_ref = refs[pos], refs[pos + 1]
        pos += 2
    out_refs = refs[pos:pos + n32 + n16]
    acc = _dot(xn_ref[...], w_ref[...])
    tn = acc.shape[1]
    for c in range(tn // LANES):
        cs = slice(c * LANES, (c + 1) * LANES)
        y = acc[:, cs]
        if has_gain:
            y = y * lax.rsqrt(jnp.mean(y * y, axis=-1, keepdims=True) + EPS) * gain_ref[:, cs]
        if rope_dim:
            y = _rope_lanes(y, cos_ref[...], sin_ref[...], rope_dim)
        for o in out_refs:
            o[:, cs] = y.astype(o.dtype)


def _proj_call(xn, w16, col0, ncols, tm, tn, *, gain=None, rope=None, rope_period=None, want32, want16, name):
    n, d = xn.shape
    assert col0 % tn == 0 and ncols % tn == 0 and n % tm == 0
    cb0 = col0 // tn
    in_specs = [pl.BlockSpec((tm, d), lambda i, j: (i, 0)),
                pl.BlockSpec((d, tn), lambda i, j: (0, cb0 + j))]
    args = [xn, w16]
    if gain is not None:
        in_specs.append(pl.BlockSpec((1, tn), lambda i, j: (0, j)))
        args.append(gain.reshape(1, ncols))
    rope_dim = 0
    if rope is not None:
        cos, sin, rope_dim = rope
        nper = rope_period // tm
        spec = pl.BlockSpec((tm, LANES), lambda i, j: (i % nper, 0))
        in_specs += [spec, spec]
        args += [cos, sin]
    out_shape, out_specs = [], []
    for dt, want in ((F32, want32), (BF16, want16)):
        if want:
            out_shape.append(jax.ShapeDtypeStruct((n, ncols), dt))
            out_specs.append(pl.BlockSpec((tm, tn), lambda i, j: (i, j)))
    vmem = 2 * (_nbytes((tm, d), BF16) + _nbytes((d, tn), BF16) + 2 * _nbytes((tm, tn), F32)) + 4 * _nbytes((tm, tn), F32)
    outs = pl.pallas_call(
        functools.partial(_proj_kernel, has_gain=gain is not None, rope_dim=rope_dim,
                          n32=int(want32), n16=int(want16)),
        out_shape=out_shape,
        grid=(n // tm, ncols // tn),
        in_specs=in_specs,
        out_specs=out_specs,
        compiler_params=_params(("parallel", "parallel"), vmem + (8 << 20)),
        name=name,
    )(*args)
    return outs


def _misc_kernel(xn_ref, w_ref, bf_ref, cos_ref, sin_ref, ki32_ref, ki16_ref, logf_ref, wi_ref):
    acc = _dot(xn_ref[...], w_ref[...])
    ki = _rope_lanes(acc[:, :LANES], cos_ref[...], sin_ref[...], D_IDX)[:, :D_IDX]
    ki32_ref[...] = ki
    ki16_ref[...] = ki.astype(BF16)
    z = acc[:, LANES:LANES + H_A] + bf_ref[...]
    logf_ref[...] = jnp.minimum(z, 0.0) - jnp.log1p(jnp.exp(-jnp.abs(z)))
    wi_ref[...] = acc[:, 2 * LANES:2 * LANES + H_IDX]


def _misc_call(xn, w_misc, b_f, cos, sin, tm, rope_period):
    n, d = xn.shape
    nper = rope_period // tm
    tspec = pl.BlockSpec((tm, LANES), lambda i: (i % nper, 0))
    return pl.pallas_call(
        _misc_kernel,
        out_shape=[jax.ShapeDtypeStruct((n, D_IDX), F32), jax.ShapeDtypeStruct((n, D_IDX), BF16),
                   jax.ShapeDtypeStruct((n, H_A), F32), jax.ShapeDtypeStruct((n, H_IDX), F32)],
        grid=(n // tm,),
        in_specs=[pl.BlockSpec((tm, d), lambda i: (i, 0)),
                  pl.BlockSpec((d, 3 * LANES), lambda i: (0, 0)),
                  pl.BlockSpec((1, H_A), lambda i: (0, 0)),
                  tspec, tspec],
        out_specs=[pl.BlockSpec((tm, D_IDX), lambda i: (i, 0)), pl.BlockSpec((tm, D_IDX), lambda i: (i, 0)),
                   pl.BlockSpec((tm, H_A), lambda i: (i, 0)), pl.BlockSpec((tm, H_IDX), lambda i: (i, 0))],
        compiler_params=_params(("parallel",), 32 << 20),
        name="proj_misc",
    )(xn, w_misc, b_f.reshape(1, H_A), cos, sin)


def _lane_prefix(x):
    lane = lax.broadcasted_iota(I32, x.shape, 1)
    k = 1
    while k < LANES:
        x = x + jnp.where(lane >= k, pltpu.roll(x, k, 1), 0.0)
        k *= 2
    return x


def _lane_suffix(x):
    lane = lax.broadcasted_iota(I32, x.shape, 1)
    k = 1
    while k < LANES:
        x = x + jnp.where(lane < LANES - k, pltpu.roll(x, LANES - k, 1), 0.0)
        k *= 2
    return x


def _cumsum_kernel(x_ref, o_ref):
    t = x_ref.shape[2]
    carry = jnp.zeros((x_ref.shape[1], 1), F32)
    for c in range(t // LANES):
        cs = slice(c * LANES, (c + 1) * LANES)
        y = _lane_prefix(x_ref[0, :, cs]) + carry
        o_ref[0, :, cs] = y
        carry = y[:, LANES - 1:LANES]


def _cumsum_call(logf_t):
    b, h, t = logf_t.shape
    return pl.pallas_call(
        _cumsum_kernel,
        out_shape=jax.ShapeDtypeStruct((b, h, t), F32),
        grid=(b,),
        in_specs=[pl.BlockSpec((1, h, t), lambda i: (i, 0, 0))],
        out_specs=pl.BlockSpec((1, h, t), lambda i: (i, 0, 0)),
        compiler_params=_params(("parallel",), 16 << 20),
        name="logf_cumsum",
    )(logf_t)


def _softmax_step(s, mask, h, hs, v, m_ref, l_ref, acc_ref):
    if mask is not None:
        s = jnp.where(mask, s, NEG)
    m_prev = m_ref[h]
    m_new = jnp.maximum(m_prev, jnp.max(s, axis=-1, keepdims=True))
    alpha = jnp.exp(m_prev - m_new)
    p = jnp.exp(s - m_new)
    if mask is not None:
        p = jnp.where(mask, p, 0.0)
    l_ref[h] = alpha * l_ref[h] + jnp.sum(p, axis=-1, keepdims=True)
    acc_ref[:, hs] = alpha * acc_ref[:, hs] + _dot(p.astype(BF16), v)
    m_ref[h] = m_new


def _init_softmax(m_ref, l_ref, acc_ref):
    m_ref[...] = jnp.full(m_ref.shape, NEG, F32)
    l_ref[...] = jnp.zeros(l_ref.shape, F32)
    acc_ref[...] = jnp.zeros(acc_ref.shape, F32)


def _fox_kernel(q_ref, k_ref, v_ref, cq_ref, ck_ref, o_ref, m_ref, l_ref, acc_ref, *, scale):
    i, j = pl.program_id(1), pl.program_id(2)
    tq, tk = q_ref.shape[0], k_ref.shape[0]

    @pl.when(j == 0)
    def _():
        _init_softmax(m_ref, l_ref, acc_ref)

    def step(diag):
        mask = None
        if diag:
            mask = lax.broadcasted_iota(I32, (tq, tk), 0) >= lax.broadcasted_iota(I32, (tq, tk), 1)
        for h in range(H_A):
            hs = slice(h * HEAD_DIM, (h + 1) * HEAD_DIM)
            s = _nt_dot(q_ref[:, hs], k_ref[:, hs]) * scale
            s = s + cq_ref[:, h:h + 1] - ck_ref[0, h:h + 1, :]
            _softmax_step(s, mask, h, hs, v_ref[:, hs], m_ref, l_ref, acc_ref)

    pl.when(j < i)(functools.partial(step, False))

    @pl.when(j == i)
    def _():
        step(True)
        for h in range(H_A):
            hs = slice(h * HEAD_DIM, (h + 1) * HEAD_DIM)
            o_ref[:, hs] = (acc_ref[:, hs] / l_ref[h]).astype(o_ref.dtype)


def _fox_call(q16, k16, v16, c_rows, c_lanes, nb, t, tq):
    nq = t // tq
    width = H_A * HEAD_DIM
    qmap = lambda b, i, j: (b * nq + i, 0)
    kmap = lambda b, i, j: (b * nq + jnp.minimum(j, i), 0)
    vmem = 2 * 4 * _nbytes((tq, width), BF16) + _nbytes((tq, width), F32) + 2 * H_A * _nbytes((tq, LANES), F32)
    return pl.pallas_call(
        functools.partial(_fox_kernel, scale=HEAD_DIM ** -0.5),
        out_shape=jax.ShapeDtypeStruct((nb * t, width), BF16),
        grid=(nb, nq, nq),
        in_specs=[pl.BlockSpec((tq, width), qmap),
                  pl.BlockSpec((tq, width), kmap),
                  pl.BlockSpec((tq, width), kmap),
                  pl.BlockSpec((tq, H_A), qmap),
                  pl.BlockSpec((1, H_A, tq), lambda b, i, j: (b, 0, jnp.minimum(j, i)))],
        out_specs=pl.BlockSpec((tq, width), qmap),
        scratch_shapes=[pltpu.VMEM((H_A, tq, 1), F32), pltpu.VMEM((H_A, tq, 1), F32),
                        pltpu.VMEM((tq, width), F32)],
        compiler_params=_params(("parallel", "parallel", "arbitrary"), vmem + (16 << 20)),
        name="fox_prompt_attention",
    )(q16, k16, v16, c_rows, c_lanes)


def _count_where(key_ref, n_chunks, chunk, rows, pred):
    def body(c, acc):
        start = pl.multiple_of(c * chunk, chunk)
        blk = key_ref[:, pl.ds(start, chunk)]
        for s in range(chunk // LANES):
            col = start + s * LANES + lax.broadcasted_iota(I32, (rows, LANES), 1)
            acc = acc + jnp.where(pred(blk[:, s * LANES:(s + 1) * LANES], col), 1.0, 0.0)
        return acc
    acc = lax.fori_loop(0, n_chunks, body, jnp.zeros((rows, LANES), F32))
    return jnp.sum(acc, axis=-1, keepdims=True)


def _select_threshold(key_ref, n_chunks, chunk, rows, n_sel, thr_ref, jcut_ref):
    nf = float(n_sel)

    def count_ge(cand):
        return _count_where(key_ref, n_chunks, chunk, rows, lambda k, col: k >= cand)

    base = jnp.where(count_ge(jnp.zeros((rows, 1), I32)) >= nf, 0, INT_MIN).astype(I32)

    def bit_body(b, base):
        cand = base | jnp.left_shift(jnp.int32(1), 30 - b)
        return jnp.where(count_ge(cand) >= nf, cand, base)

    base = lax.fori_loop(0, 31, bit_body, base)
    n_ge = count_ge(base)
    n_gt = count_ge(base + 1)
    need = jnp.logical_and(n_ge > nf, base != NEG_INF_KEY)
    thr_ref[...] = _key_to_f32(base).reshape(thr_ref.shape)
    jcut_ref[...] = jnp.full(jcut_ref.shape, INT_MAX, I32)

    @pl.when(jnp.max(jnp.where(need, 1.0, 0.0)) > 0.0)
    def _():
        want = nf - n_gt

        def count_eq_below(x):
            return _count_where(key_ref, n_chunks, chunk, rows,
                                lambda k, col: jnp.logical_and(k == base, col < x))

        def col_body(b, x):
            cand = x | jnp.left_shift(jnp.int32(1), 14 - b)
            return jnp.where(count_eq_below(cand) < want, cand, x)

        x = lax.fori_loop(0, 15, col_body, jnp.zeros((rows, 1), I32))
        jcut_ref[...] = jnp.where(need, x, INT_MAX).reshape(jcut_ref.shape)


def _canon_zero(x):
    return jnp.where(x == 0.0, 0.0, x)


def _idx_kernel(qi_ref, wi_ref, ki_ref, sc_ref, thr_ref, jcut_ref, key_ref, *, n_sel, idx_scale):
    i, j = pl.program_id(1), pl.program_id(2)
    tq, tk = qi_ref.shape[0], ki_ref.shape[0]

    def scores(diag):
        acc = jnp.zeros((tq, tk), F32)
        for h in range(H_IDX):
            d = _nt_dot(qi_ref[:, h * D_IDX:(h + 1) * D_IDX], ki_ref[...])
            acc = acc + wi_ref[:, h:h + 1] * jnp.maximum(d, 0.0)
        sc = _canon_zero(acc * idx_scale)
        if diag:
            causal = lax.broadcasted_iota(I32, (tq, tk), 0) >= lax.broadcasted_iota(I32, (tq, tk), 1)
            sc = jnp.where(causal, sc, -jnp.inf)
        sc_ref[...] = sc
        key_ref[:, pl.ds(pl.multiple_of(j * tk, tk), tk)] = _f32_to_key(sc)

    pl.when(j < i)(functools.partial(scores, False))

    @pl.when(j == i)
    def _():
        scores(True)
        _select_threshold(key_ref, i + 1, tk, tq, n_sel, thr_ref, jcut_ref)


def _idx_call(qi16, wi, ki16, nb, t, tq, n_sel):
    nq = t // tq
    qmap = lambda b, i, j: (b * nq + i, 0)
    vmem = _nbytes((tq, t), I32) + 2 * (_nbytes((tq, H_IDX * D_IDX), BF16) + 2 * _nbytes((tq, tq), F32))
    return pl.pallas_call(
        functools.partial(_idx_kernel, n_sel=n_sel, idx_scale=(H_IDX * D_IDX) ** -0.5),
        out_shape=[jax.ShapeDtypeStruct((nb * t, t), F32),
                   jax.ShapeDtypeStruct((nb * t, 1), F32),
                   jax.ShapeDtypeStruct((nb * t, 1), I32)],
        grid=(nb, nq, nq),
        in_specs=[pl.BlockSpec((tq, H_IDX * D_IDX), qmap),
                  pl.BlockSpec((tq, H_IDX), qmap),
                  pl.BlockSpec((tq, D_IDX), lambda b, i, j: (b * nq + jnp.minimum(j, i), 0))],
        out_specs=[pl.BlockSpec((tq, tq), lambda b, i, j: (b * nq + i, jnp.minimum(j, i))),
                   pl.BlockSpec((tq, 1), qmap),
                   pl.BlockSpec((tq, 1), qmap)],
        scratch_shapes=[pltpu.VMEM((tq, t), I32)],
        compiler_params=_params(("parallel", "parallel", "arbitrary"), vmem + (16 << 20)),
        name="dsa_prompt_indexer",
    )(qi16, wi, ki16)


def _selected(sc, thr, jcut, col):
    return jnp.logical_or(sc > thr, jnp.logical_and(sc == thr, col <= jcut))


def _dsa_kernel(q_ref, k_ref, v_ref, sc_ref, thr_ref, jcut_ref, o_ref, m_ref, l_ref, acc_ref, *, scale):
    i, j = pl.program_id(1), pl.program_id(2)
    tq, tk = q_ref.shape[0], k_ref.shape[0]

    @pl.when(j == 0)
    def _():
        _init_softmax(m_ref, l_ref, acc_ref)

    @pl.when(j <= i)
    def _():
        row = i * tq + lax.broadcasted_iota(I32, (tq, tk), 0)
        col = j * tk + lax.broadcasted_iota(I32, (tq, tk), 1)
        mask = jnp.logical_and(row >= col, _selected(sc_ref[...], thr_ref[...], jcut_ref[...], col))
        for h in range(H_B):
            hs = slice(h * HEAD_DIM, (h + 1) * HEAD_DIM)
            g = h // GROUP_B
            gs = slice(g * HEAD_DIM, (g + 1) * HEAD_DIM)
            s = _nt_dot(q_ref[:, hs], k_ref[:, gs]) * scale
            _softmax_step(s, mask, h, hs, v_ref[:, gs], m_ref, l_ref, acc_ref)

    @pl.when(j == i)
    def _():
        for h in range(H_B):
            hs = slice(h * HEAD_DIM, (h + 1) * HEAD_DIM)
            o_ref[:, hs] = (acc_ref[:, hs] / l_ref[h]).astype(o_ref.dtype)


def _dsa_call(q16, k16, v16, sc, thr, jcut, nb, t, tq):
    nq = t // tq
    qw, kw = H_B * HEAD_DIM, KV_B * HEAD_DIM
    qmap = lambda b, i, j: (b * nq + i, 0)
    kmap = lambda b, i, j: (b * nq + jnp.minimum(j, i), 0)
    vmem = 2 * (2 * _nbytes((tq, qw), BF16) + 2 * _nbytes((tq, kw), BF16) + _nbytes((tq, tq), F32)) \
        + _nbytes((tq, qw), F32) + 2 * H_B * _nbytes((tq, LANES), F32)
    return pl.pallas_call(
        functools.partial(_dsa_kernel, scale=HEAD_DIM ** -0.5),
        out_shape=jax.ShapeDtypeStruct((nb * t, qw), BF16),
        grid=(nb, nq, nq),
        in_specs=[pl.BlockSpec((tq, qw), qmap),
                  pl.BlockSpec((tq, kw), kmap),
                  pl.BlockSpec((tq, kw), kmap),
                  pl.BlockSpec((tq, tq), lambda b, i, j: (b * nq + i, jnp.minimum(j, i))),
                  pl.BlockSpec((tq, 1), qmap),
                  pl.BlockSpec((tq, 1), qmap)],
        out_specs=pl.BlockSpec((tq, qw), qmap),
        scratch_shapes=[pltpu.VMEM((H_B, tq, 1), F32), pltpu.VMEM((H_B, tq, 1), F32),
                        pltpu.VMEM((tq, qw), F32)],
        compiler_params=_params(("parallel", "parallel", "arbitrary"), vmem + (16 << 20)),
        name="dsa_prompt_attention",
    )(q16, k16, v16, sc, thr, jcut)


PAGES_PER_STEP = 4


def _rows_from_heads(x, reps):
    return jnp.concatenate([jnp.broadcast_to(x[h:h + 1, :], (reps, x.shape[1])) for h in range(x.shape[0])], axis=0)


def _decode_softmax_step(s, mask, v, m_ref, l_ref, acc_ref):
    if mask is not None:
        s = jnp.where(mask, s, NEG)
    m_prev = m_ref[...]
    m_new = jnp.maximum(m_prev, jnp.max(s, axis=-1, keepdims=True))
    alpha = jnp.exp(m_prev - m_new)
    p = jnp.exp(s - m_new)
    if mask is not None:
        p = jnp.where(mask, p, 0.0)
    l_ref[...] = alpha * l_ref[...] + jnp.sum(p, axis=-1, keepdims=True)
    acc_ref[...] = alpha * acc_ref[...] + _dot(p.astype(BF16), v)
    m_ref[...] = m_new


def _fox_decode_kernel(pt_ref, q_ref, lfq_ref, knew_ref, vnew_ref, lfnew_ref, *rest, n_tok, scale):
    del pt_ref
    npg = PAGES_PER_STEP
    k_refs, v_refs, lf_refs = rest[:npg], rest[npg:2 * npg], rest[2 * npg:3 * npg]
    o_ref, m_ref, l_ref, acc_ref, carry_ref, rowc_ref = rest[3 * npg:]
    s_id = pl.program_id(1)
    rows = H_A * n_tok

    def attend(k16, v16, key_bias, mask):
        s = _nt_dot(q_ref[0], k16) * scale
        s = s + rowc_ref[...] + _rows_from_heads(key_bias, n_tok)
        _decode_softmax_step(s, mask, v16, m_ref, l_ref, acc_ref)

    @pl.when(s_id == 0)
    def _():
        _init_softmax(m_ref, l_ref, acc_ref)
        lf = lfq_ref[...]
        tok = lax.broadcasted_iota(I32, lf.shape, 0)
        after = jnp.zeros(lf.shape, F32)
        for jj in range(1, n_tok):
            after = after + jnp.where(tok < jj, lf[jj:jj + 1, :], 0.0)
        rowc_ref[...] = jnp.concatenate([-after[:, h:h + 1] for h in range(H_A)], axis=0)
        lfn = lfnew_ref[0]
        incl = _lane_suffix(lfn)
        tokq = lax.broadcasted_iota(I32, (rows, LANES), 0) % n_tok
        mask = lax.broadcasted_iota(I32, (rows, LANES), 1) <= tokq
        attend(knew_ref[0], vnew_ref[0], incl - lfn, mask)
        carry_ref[...] = incl[:, 0:1]

    @pl.when(s_id > 0)
    def _():
        for p in reversed(range(npg)):
            lf = lf_refs[p][0]
            incl = _lane_suffix(lf)
            attend(k_refs[p][0].astype(BF16), v_refs[p][0].astype(BF16),
                   incl - lf + carry_ref[...], None)
            carry_ref[...] = carry_ref[...] + incl[:, 0:1]

    @pl.when(s_id == pl.num_programs(1) - 1)
    def _():
        for h in range(H_A):
            rs = slice(h * n_tok, (h + 1) * n_tok)
            hs = slice(h * HEAD_DIM, (h + 1) * HEAD_DIM)
            o_ref[:, hs] = (acc_ref[rs, hs] / l_ref[rs, :]).astype(o_ref.dtype)


def _page_specs(block, n_groups, reverse):
    specs = []
    for p in range(PAGES_PER_STEP):
        def imap(b, s, pt, p=p):
            g = jnp.maximum(s - 1, 0)
            if reverse:
                g = n_groups - 1 - g
            return (pt[b, g * PAGES_PER_STEP + p],) + (0,) * (len(block) - 1)
        specs.append(pl.BlockSpec(block, imap))
    return specs


def _fox_decode_call(page_table, qbd, lf_rows, knew, vnew, lfnew, k_pool, v_pool, lf_pool, n_tok):
    nb, n_pages = page_table.shape
    n_groups = n_pages // PAGES_PER_STEP
    rows, width = H_A * n_tok, H_A * HEAD_DIM
    bmap = lambda b, s, pt: (b, 0, 0)
    in_specs = [pl.BlockSpec((1, rows, width), bmap),
                pl.BlockSpec((n_tok, H_A), lambda b, s, pt: (b, 0)),
                pl.BlockSpec((1, PAGE_SIZE, width), bmap),
                pl.BlockSpec((1, PAGE_SIZE, width), bmap),
                pl.BlockSpec((1, H_A, PAGE_SIZE), bmap)]
    in_specs += _page_specs((1, PAGE_SIZE, width), n_groups, True)
    in_specs += _page_specs((1, PAGE_SIZE, width), n_groups, True)
    in_specs += _page_specs((1, H_A, PAGE_SIZE), n_groups, True)
    vmem = 2 * 2 * PAGES_PER_STEP * _nbytes((PAGE_SIZE, width), F32) + 4 * _nbytes((rows, width), F32)
    return pl.pallas_call(
        functools.partial(_fox_decode_kernel, n_tok=n_tok, scale=HEAD_DIM ** -0.5),
        out_shape=jax.ShapeDtypeStruct((nb * n_tok, width), BF16),
        grid_spec=pltpu.PrefetchScalarGridSpec(
            num_scalar_prefetch=1,
            grid=(nb, n_groups + 1),
            in_specs=in_specs,
            out_specs=pl.BlockSpec((n_tok, width), lambda b, s, pt: (b, 0)),
            scratch_shapes=[pltpu.VMEM((rows, 1), F32), pltpu.VMEM((rows, 1), F32),
                            pltpu.VMEM((rows, width), F32), pltpu.VMEM((H_A, 1), F32),
                            pltpu.VMEM((rows, 1), F32)]),
        compiler_params=_params(("parallel", "arbitrary"), vmem + (16 << 20)),
        name="fox_decode_attention",
    )(page_table, qbd, lf_rows, knew, vnew, lfnew,
      *([k_pool] * PAGES_PER_STEP), *([v_pool] * PAGES_PER_STEP), *([lf_pool] * PAGES_PER_STEP))


def _idx_decode_kernel(pt_ref, qi_ref, wi_ref, kinew_ref, *rest, n_tok, n_sel, idx_scale):
    del pt_ref
    npg = PAGES_PER_STEP
    ki_refs = rest[:npg]
    sc_ref, thr_ref, jcut_ref, key_ref = rest[npg:]
    s_id = pl.program_id(1)
    last = pl.num_programs(1) - 1
    width = npg * PAGE_SIZE

    def page_scores(ki16):
        d = jnp.maximum(_nt_dot(qi_ref[0], ki16), 0.0) * wi_ref[0]
        acc = jnp.zeros((n_tok, PAGE_SIZE), F32)
        for h in range(H_IDX):
            acc = acc + d[h * n_tok:(h + 1) * n_tok, :]
        return _canon_zero(acc * idx_scale)

    @pl.when(s_id < last)
    def _():
        for p in range(npg):
            sc = page_scores(ki_refs[p][0].astype(BF16))
            sc_ref[0, :, p * PAGE_SIZE:(p + 1) * PAGE_SIZE] = sc
            key_ref[:, pl.ds(pl.multiple_of(s_id * width + p * PAGE_SIZE, PAGE_SIZE), PAGE_SIZE)] = _f32_to_key(sc)

    @pl.when(s_id == last)
    def _():
        sc = page_scores(kinew_ref[0])
        causal = lax.broadcasted_iota(I32, sc.shape, 1) <= lax.broadcasted_iota(I32, sc.shape, 0)
        sc = jnp.where(causal, sc, -jnp.inf)
        pad = jnp.full((n_tok, PAGE_SIZE), -jnp.inf, F32)
        base_col = pl.multiple_of(s_id * width, width)
        sc_ref[0, :, 0:PAGE_SIZE] = sc
        key_ref[:, pl.ds(base_col, PAGE_SIZE)] = _f32_to_key(sc)
        for p in range(1, npg):
            sc_ref[0, :, p * PAGE_SIZE:(p + 1) * PAGE_SIZE] = pad
            key_ref[:, pl.ds(base_col + p * PAGE_SIZE, PAGE_SIZE)] = _f32_to_key(pad)
        _select_threshold(key_ref, s_id + 1, width, n_tok, n_sel, thr_ref, jcut_ref)


def _idx_decode_call(page_table, qi_rows, wi_rows, kinew, ki_pool, n_tok, n_sel):
    nb, n_pages = page_table.shape
    n_groups = n_pages // PAGES_PER_STEP
    width = PAGES_PER_STEP * PAGE_SIZE
    n_cols = (n_groups + 1) * width
    bmap = lambda b, s, pt: (b, 0, 0)

    def page_map(p):
        return lambda b, s, pt: (pt[b, jnp.minimum(s, n_groups - 1) * PAGES_PER_STEP + p], 0, 0)

    in_specs = [pl.BlockSpec((1, H_IDX * n_tok, D_IDX), bmap),
                pl.BlockSpec((1, H_IDX * n_tok, 1), bmap),
                pl.BlockSpec((1, PAGE_SIZE, D_IDX), bmap)]
    in_specs += [pl.BlockSpec((1, PAGE_SIZE, D_IDX), page_map(p)) for p in range(PAGES_PER_STEP)]
    return pl.pallas_call(
        functools.partial(_idx_decode_kernel, n_tok=n_tok, n_sel=n_sel, idx_scale=(H_IDX * D_IDX) ** -0.5),
        out_shape=[jax.ShapeDtypeStruct((nb, n_tok, n_cols), F32),
                   jax.ShapeDtypeStruct((nb, n_tok, 1), F32),
                   jax.ShapeDtypeStruct((nb, n_tok, 1), I32)],
        grid_spec=pltpu.PrefetchScalarGridSpec(
            num_scalar_prefetch=1,
            grid=(nb, n_groups + 1),
            in_specs=in_specs,
            out_specs=[pl.BlockSpec((1, n_tok, width), lambda b, s, pt: (b, 0, s)),
                       pl.BlockSpec((1, n_tok, 1), bmap),
                       pl.BlockSpec((1, n_tok, 1), bmap)],
            scratch_shapes=[pltpu.VMEM((n_tok, n_cols), I32)]),
        compiler_params=_params(("parallel", "arbitrary"), 24 << 20),
        name="dsa_decode_indexer",
    )(page_table, qi_rows, wi_rows, kinew, *([ki_pool] * PAGES_PER_STEP))


def _dsa_decode_kernel(pt_ref, q_ref, knew_ref, vnew_ref, sc_ref, thr_ref, jcut_ref, *rest, n_tok, scale):
    del pt_ref
    npg = PAGES_PER_STEP
    k_refs, v_refs = rest[:npg], rest[npg:2 * npg]
    o_ref, m_ref, l_ref, acc_ref = rest[2 * npg:]
    s_id = pl.program_id(1)
    last = pl.num_programs(1) - 1
    width = npg * PAGE_SIZE

    @pl.when(s_id == 0)
    def _():
        _init_softmax(m_ref, l_ref, acc_ref)

    def attend(k16, v16, p, extra):
        col = s_id * width + p * PAGE_SIZE + lax.broadcasted_iota(I32, (n_tok, PAGE_SIZE), 1)
        sel = _selected(sc_ref[0, :, p * PAGE_SIZE:(p + 1) * PAGE_SIZE], thr_ref[0], jcut_ref[0], col)
        if extra is not None:
            sel = jnp.logical_and(sel, extra)
        mask = jnp.concatenate([jnp.where(sel, 1.0, 0.0)] * H_B, axis=0) > 0.5
        _decode_softmax_step(_nt_dot(q_ref[0], k16) * scale, mask, v16, m_ref, l_ref, acc_ref)

    @pl.when(s_id < last)
    def _():
        for p in range(npg):
            attend(k_refs[p][0].astype(BF16), v_refs[p][0].astype(BF16), p, None)

    @pl.when(s_id == last)
    def _():
        causal = lax.broadcasted_iota(I32, (n_tok, PAGE_SIZE), 1) <= lax.broadcasted_iota(I32, (n_tok, PAGE_SIZE), 0)
        attend(knew_ref[0], vnew_ref[0], 0, causal)
        for h in range(H_B):
            rs = slice(h * n_tok, (h + 1) * n_tok)
            g = h // GROUP_B
            o_ref[:, h * HEAD_DIM:(h + 1) * HEAD_DIM] = (
                acc_ref[rs, g * HEAD_DIM:(g + 1) * HEAD_DIM] / l_ref[rs, :]).astype(o_ref.dtype)


def _dsa_decode_call(page_table, qbd, knew, vnew, sc, thr, jcut, k_pool, v_pool, n_tok):
    nb, n_pages = page_table.shape
    n_groups = n_pages // PAGES_PER_STEP
    width = PAGES_PER_STEP * PAGE_SIZE
    rows, kw = H_B * n_tok, KV_B * HEAD_DIM
    bmap = lambda b, s, pt: (b, 0, 0)

    def page_map(p):
        return lambda b, s, pt: (pt[b, jnp.minimum(s, n_groups - 1) * PAGES_PER_STEP + p], 0, 0)

    in_specs = [pl.BlockSpec((1, rows, kw), bmap),
                pl.BlockSpec((1, PAGE_SIZE, kw), bmap),
                pl.BlockSpec((1, PAGE_SIZE, kw), bmap),
                pl.BlockSpec((1, n_tok, width), lambda b, s, pt: (b, 0, s)),
                pl.BlockSpec((1, n_tok, 1), bmap),
                pl.BlockSpec((1, n_tok, 1), bmap)]
    in_specs += [pl.BlockSpec((1, PAGE_SIZE, kw), page_map(p)) for p in range(PAGES_PER_STEP)]
    in_specs += [pl.BlockSpec((1, PAGE_SIZE, kw), page_map(p)) for p in range(PAGES_PER_STEP)]
    return pl.pallas_call(
        functools.partial(_dsa_decode_kernel, n_tok=n_tok, scale=HEAD_DIM ** -0.5),
        out_shape=jax.ShapeDtypeStruct((nb * n_tok, H_B * HEAD_DIM), BF16),
        grid_spec=pltpu.PrefetchScalarGridSpec(
            num_scalar_prefetch=1,
            grid=(nb, n_groups + 1),
            in_specs=in_specs,
            out_specs=pl.BlockSpec((n_tok, H_B * HEAD_DIM), lambda b, s, pt: (b, 0)),
            scratch_shapes=[pltpu.VMEM((rows, 1), F32), pltpu.VMEM((rows, 1), F32),
                            pltpu.VMEM((rows, kw), F32)]),
        compiler_params=_params(("parallel", "arbitrary"), 24 << 20),
        name="dsa_decode_attention",
    )(page_table, qbd, knew, vnew, sc, thr, jcut, *([k_pool] * PAGES_PER_STEP), *([v_pool] * PAGES_PER_STEP))


def _outproj_kernel(oa_ref, ob_ref, w_ref, x_ref, h_ref):
    ka = oa_ref.shape[1]
    h_ref[...] = x_ref[...] + _dot(oa_ref[...], w_ref[:ka, :]) + _dot(ob_ref[...], w_ref[ka:, :])


def _outproj_call(oa, ob, w16, x, tm, tn):
    n, d = x.shape
    ka, kb = oa.shape[1], ob.shape[1]
    vmem = 2 * (_nbytes((tm, ka + kb), BF16) + _nbytes((ka + kb, tn), BF16) + 2 * _nbytes((tm, tn), F32))
    return pl.pallas_call(
        _outproj_kernel,
        out_shape=jax.ShapeDtypeStruct((n, d), F32),
        grid=(n // tm, d // tn),
        in_specs=[pl.BlockSpec((tm, ka), lambda i, j: (i, 0)),
                  pl.BlockSpec((tm, kb), lambda i, j: (i, 0)),
                  pl.BlockSpec((ka + kb, tn), lambda i, j: (0, j)),
                  pl.BlockSpec((tm, tn), lambda i, j: (i, j))],
        out_specs=pl.BlockSpec((tm, tn), lambda i, j: (i, j)),
        compiler_params=_params(("parallel", "parallel"), vmem + (8 << 20)),
        name="out_proj",
    )(oa, ob, w16, x)


HALO = SUBLANES


def _shift_rows(g, k, row, fill):
    out = pltpu.roll(g, k, 0)
    for r in range(k):
        out = jnp.where(row == r, fill(r), out)
    return out


def _ffn_kernel(*refs, seq_rows, has_state):
    if has_state:
        (h_ref, gn_ref, wg_ref, wu_ref, cw_ref, cb_ref, wd_ref, s1_ref, s2_ref,
         y_ref, gl_ref, hn_ref, acc_ref) = refs
    else:
        (h_ref, gn_ref, wg_ref, wu_ref, cw_ref, cb_ref, wd_ref,
         y_ref, gl_ref, hn_ref, acc_ref, halo_ref) = refs
    i, j = pl.program_id(0), pl.program_id(1)
    tm, tf = h_ref.shape[0], wg_ref.shape[1]

    @pl.when(j == 0)
    def _():
        x = h_ref[...]
        y = x * lax.rsqrt(jnp.mean(x * x, axis=-1, keepdims=True) + EPS)
        hn_ref[...] = (y * gn_ref[...]).astype(BF16)
        acc_ref[...] = jnp.zeros(acc_ref.shape, F32)

    hn = hn_ref[...]
    g = _dot(hn, wg_ref[...])
    u = _dot(hn, wu_ref[...])
    row = lax.broadcasted_iota(I32, (tm, tf), 0)
    if has_state:
        tpos = row % seq_rows
        gm1 = jnp.where(tpos >= 1, pltpu.roll(g, 1, 0), s1_ref[...])
        gm2 = jnp.where(tpos >= 2, pltpu.roll(g, 2, 0), s2_ref[...])
        gl_ref[...] = g
    else:
        tiles_per_seq = seq_rows // tm

        @pl.when(i % tiles_per_seq == 0)
        def _():
            halo_ref[j] = jnp.zeros((HALO, tf), F32)

        prev = halo_ref[j]
        gm1 = _shift_rows(g, 1, row, lambda r: prev[HALO - 1 + r:HALO + r, :])
        gm2 = _shift_rows(g, 2, row, lambda r: prev[HALO - 2 + r:HALO - 1 + r, :])
        halo_ref[j] = g[tm - HALO:, :]
        gl_ref[0] = g[tm - HALO:, :]
    gc = cb_ref[...] + cw_ref[0:1, :] * gm2 + cw_ref[1:2, :] * gm1 + cw_ref[2:3, :] * g
    act = gc * (1.0 / (1.0 + jnp.exp(-gc))) * u
    acc_ref[...] += _dot(act.astype(BF16), wd_ref[...])

    @pl.when(j == pl.num_programs(1) - 1)
    def _():
        y_ref[...] = h_ref[...] + acc_ref[...]


def _ffn_call(h, g_ffn, wg16, wu16, cw, cb, wd16, tm, tf, seq_rows, state=None):
    n, d = h.shape
    dff = wg16.shape[1]
    nj = dff // tf
    has_state = state is not None
    in_specs = [pl.BlockSpec((tm, d), lambda i, j: (i, 0)),
                pl.BlockSpec((1, d), lambda i, j: (0, 0)),
                pl.BlockSpec((d, tf), lambda i, j: (0, j)),
                pl.BlockSpec((d, tf), lambda i, j: (0, j)),
                pl.BlockSpec((CONV_W, tf), lambda i, j: (0, j)),
                pl.BlockSpec((1, tf), lambda i, j: (0, j)),
                pl.BlockSpec((tf, d), lambda i, j: (j, 0))]
    args = [h, g_ffn.reshape(1, d), wg16, wu16, cw, cb.reshape(1, dff), wd16]
    scratch = [pltpu.VMEM((tm, d), BF16), pltpu.VMEM((tm, d), F32)]
    if has_state:
        in_specs += [pl.BlockSpec((tm, tf), lambda i, j: (i, j))] * 2
        args += list(state)
        gl_shape = jax.ShapeDtypeStruct((n, dff), F32)
        gl_spec = pl.BlockSpec((tm, tf), lambda i, j: (i, j))
    else:
        assert seq_rows % tm == 0
        gl_shape = jax.ShapeDtypeStruct((n // tm, HALO, dff), F32)
        gl_spec = pl.BlockSpec((1, HALO, tf), lambda i, j: (i, 0, j))
        scratch.append(pltpu.VMEM((nj, HALO, tf), F32))
    vmem = 2 * (2 * _nbytes((tm, d), F32) + 3 * _nbytes((d, tf), BF16)) + _nbytes((tm, d), BF16) \
        + _nbytes((tm, d), F32) + 6 * _nbytes((tm, tf), F32)
    return pl.pallas_call(
        functools.partial(_ffn_kernel, seq_rows=seq_rows, has_state=has_state),
        out_shape=[jax.ShapeDtypeStruct((n, d), F32), gl_shape],
        grid=(n // tm, nj),
        in_specs=in_specs,
        out_specs=[pl.BlockSpec((tm, d), lambda i, j: (i, 0)), gl_spec],
        scratch_shapes=scratch,
        compiler_params=_params(("arbitrary", "arbitrary"), vmem + (8 << 20)),
        name="conv_ffn_state" if has_state else "conv_ffn",
    )(*args)


def _rope_tables(pos, dim):
    half = dim // 2
    inv = ROPE_THETA ** (-jnp.arange(half, dtype=F32) / half)
    ang = pos.astype(F32)[:, None] * inv[None, :]
    cos = jnp.tile(jnp.concatenate([jnp.cos(ang), jnp.cos(ang)], axis=-1), (1, LANES // dim))
    sin = jnp.tile(jnp.concatenate([-jnp.sin(ang), jnp.sin(ang)], axis=-1), (1, LANES // dim))
    return cos, sin


def _pad_to(x, size, axis):
    pad = [(0, 0)] * x.ndim
    pad[axis] = (0, size - x.shape[axis])
    return jnp.pad(x, pad)


def _project(x2d, wts, tables, tm, rope_period):
    xn = _rms_call(x2d, wts["g_attn"], tm)
    w16 = wts["w_main"]
    (cos_h, sin_h), (cos_i, sin_i) = tables
    hw = H_A * HEAD_DIM
    kw = KV_B * HEAD_DIM
    call = functools.partial(_proj_call, xn, w16, tm=tm)
    (qa16,) = call(0, hw, tn=512, gain=jnp.tile(wts["g_qa"], H_A), want32=False, want16=True, name="proj_qa")
    ka32, ka16 = call(hw, hw, tn=512, gain=jnp.tile(wts["g_ka"], H_A), want32=True, want16=True, name="proj_ka")
    va32, va16 = call(2 * hw, hw, tn=512, want32=True, want16=True, name="proj_va")
    (qb16,) = call(3 * hw, hw, tn=512, gain=jnp.tile(wts["g_qb"], H_B), rope=(cos_h, sin_h, HEAD_DIM),
                   rope_period=rope_period, want32=False, want16=True, name="proj_qb")
    kb32, kb16 = call(4 * hw, kw, tn=kw, gain=jnp.tile(wts["g_kb"], KV_B), rope=(cos_h, sin_h, HEAD_DIM),
                      rope_period=rope_period, want32=True, want16=True, name="proj_kb")
    vb32, vb16 = call(4 * hw + kw, kw, tn=kw, want32=True, want16=True, name="proj_vb")
    (qi16,) = call(4 * hw + 2 * kw, H_IDX * D_IDX, tn=512, rope=(cos_i, sin_i, D_IDX),
                   rope_period=rope_period, want32=False, want16=True, name="proj_qi")
    ki32, ki16, logf, wi = _misc_call(xn, wts["w_misc"], wts["b_f"], cos_i, sin_i, tm, rope_period)
    return dict(qa16=qa16, ka32=ka32, ka16=ka16, va32=va32, va16=va16, qb16=qb16, kb32=kb32, kb16=kb16,
                vb32=vb32, vb16=vb16, qi16=qi16, ki32=ki32, ki16=ki16, logf=logf, wi=wi)


def _layer_weights(l, w_in, b_f, g_qa, g_ka, g_qb, g_kb, g_attn, w_out, g_ffn, w_gate, w_up, conv_w, conv_b, w_down):
    hw, kw = H_A * HEAD_DIM, KV_B * HEAD_DIM
    splits = (hw, hw, hw, H_A, H_B * HEAD_DIM, kw, kw, H_IDX * D_IDX, D_IDX, H_IDX)
    pts = [0] + [int(v) for v in np.cumsum(splits)]
    col = lambda k: w_in[l][:, pts[k]:pts[k + 1]]
    w_main = jnp.concatenate([col(0), col(1), col(2), col(4), col(5), col(6), col(7)], axis=1).astype(BF16)
    w_misc = jnp.concatenate([_pad_to(col(8), LANES, 1), _pad_to(col(3), LANES, 1), _pad_to(col(9), LANES, 1)],
                             axis=1).astype(BF16)
    dff = w_gate.shape[2]
    dffp = -(-dff // 512) * 512
    return dict(
        w_main=w_main, w_misc=w_misc, b_f=b_f[l], g_qa=g_qa[l], g_ka=g_ka[l], g_qb=g_qb[l], g_kb=g_kb[l],
        g_attn=g_attn[l], w_out=w_out[l].astype(BF16), g_ffn=g_ffn[l],
        w_gate=_pad_to(w_gate[l], dffp, 1).astype(BF16), w_up=_pad_to(w_up[l], dffp, 1).astype(BF16),
        conv_w=_pad_to(conv_w[l], dffp, 1), conv_b=_pad_to(conv_b[l], dffp, 0),
        w_down=_pad_to(w_down[l], dffp, 0).astype(BF16), dff=dff)


def _prompt_layer(x, wts, tables):
    nb, t, d = x.shape
    x2d = x.reshape(nb * t, d)
    pr = _project(x2d, wts, tables, tm=512, rope_period=t)
    logf_t = jnp.swapaxes(pr["logf"].reshape(nb, t, H_A), 1, 2)
    c_lanes = _cumsum_call(logf_t)
    c_rows = jnp.swapaxes(c_lanes, 1, 2).reshape(nb * t, H_A)
    oa = _fox_call(pr["qa16"], pr["ka16"], pr["va16"], c_rows, c_lanes, nb, t, tq=512)
    n_sel = min(TOPK_MAX, t // 4)
    sc, thr, jcut = _idx_call(pr["qi16"], pr["wi"], pr["ki16"], nb, t, tq=256, n_sel=n_sel)
    ob = _dsa_call(pr["qb16"], pr["kb16"], pr["vb16"], sc, thr, jcut, nb, t, tq=256)
    h = _outproj_call(oa, ob, wts["w_out"], x2d, tm=512, tn=512)
    y, glast = _ffn_call(h, wts["g_ffn"], wts["w_gate"], wts["w_up"], wts["conv_w"], wts["conv_b"], wts["w_down"],
                         tm=512, tf=512, seq_rows=t)
    dff = wts["dff"]
    new_g = glast.reshape(nb, t // 512, HALO, -1)[:, -1, HALO - (CONV_W - 1):, :dff]
    rows = (pr["ka32"].reshape(nb, t, H_A, HEAD_DIM), pr["va32"].reshape(nb, t, H_A, HEAD_DIM),
            pr["logf"].reshape(nb, t, H_A), pr["kb32"].reshape(nb, t, KV_B, HEAD_DIM),
            pr["vb32"].reshape(nb, t, KV_B, HEAD_DIM), pr["ki32"].reshape(nb, t, D_IDX), new_g)
    return y.reshape(nb, t, d), rows


def _block_diag_rows(q, n_blocks):
    nb, heads, tok, hd = q.shape
    per = heads // n_blocks
    blk = jnp.arange(heads) // per
    onehot = (blk[:, None] == jnp.arange(n_blocks)[None, :])
    out = jnp.where(onehot[None, :, None, :, None], q[:, :, :, None, :], jnp.zeros((), q.dtype))
    return out.reshape(nb, heads * tok, n_blocks * hd)


def _sample_layer(x, wts, tables, pools, state, page_table):
    nb, nt, d = x.shape
    assert nt == SUBLANES
    n = nb * nt
    x2d = x.reshape(n, d)
    pr = _project(x2d, wts, tables, tm=n, rope_period=n)
    fox_k, fox_v, fox_lf, dsa_k, dsa_v, idx_k = pools
    n_pool = fox_k.shape[0]
    past_len = page_table.shape[1] * PAGE_SIZE

    def heads_first(a, heads, hd):
        return jnp.swapaxes(a.reshape(nb, nt, heads, hd), 1, 2)

    def new_tile(a):
        return _pad_to(a.reshape(nb, nt, -1), PAGE_SIZE, 1)

    qbd = _block_diag_rows(heads_first(pr["qa16"], H_A, HEAD_DIM), H_A)
    lfnew = _pad_to(jnp.swapaxes(pr["logf"].reshape(nb, nt, H_A), 1, 2), PAGE_SIZE, 2)
    oa = _fox_decode_call(page_table, qbd, pr["logf"], new_tile(pr["ka16"]), new_tile(pr["va16"]), lfnew,
                          fox_k.reshape(n_pool, PAGE_SIZE, H_A * HEAD_DIM),
                          fox_v.reshape(n_pool, PAGE_SIZE, H_A * HEAD_DIM),
                          jnp.swapaxes(fox_lf, 1, 2), nt)
    n_sel = min(TOPK_MAX, (past_len + nt) // 4)
    qi_rows = heads_first(pr["qi16"], H_IDX, D_IDX).reshape(nb, H_IDX * nt, D_IDX)
    wi_rows = jnp.swapaxes(pr["wi"].reshape(nb, nt, H_IDX), 1, 2).reshape(nb, H_IDX * nt, 1)
    sc, thr, jcut = _idx_decode_call(page_table, qi_rows, wi_rows, new_tile(pr["ki16"]), idx_k, nt, n_sel)
    qbd_b = _block_diag_rows(heads_first(pr["qb16"], H_B, HEAD_DIM), KV_B)
    ob = _dsa_decode_call(page_table, qbd_b, new_tile(pr["kb16"]), new_tile(pr["vb16"]), sc, thr, jcut,
                          dsa_k.reshape(n_pool, PAGE_SIZE, KV_B * HEAD_DIM),
                          dsa_v.reshape(n_pool, PAGE_SIZE, KV_B * HEAD_DIM), nt)
    h = _outproj_call(oa, ob, wts["w_out"], x2d, tm=n, tn=512)
    dffp = wts["w_gate"].shape[1]
    st = _pad_to(state, dffp, 2)
    zeros = jnp.zeros((nb, nt, dffp), F32)
    s1 = zeros.at[:, 0].set(st[:, 1]).reshape(n, dffp)
    s2 = zeros.at[:, 0].set(st[:, 0]).at[:, 1].set(st[:, 1]).reshape(n, dffp)
    y, gfull = _ffn_call(h, wts["g_ffn"], wts["w_gate"], wts["w_up"], wts["conv_w"], wts["conv_b"], wts["w_down"],
                         tm=n, tf=512, seq_rows=nt, state=(s1, s2))
    dff = wts["dff"]
    new_g = gfull.reshape(nb, nt, dffp)[:, nt - (CONV_W - 1):, :dff]
    rows = (pr["ka32"].reshape(nb, nt, H_A, HEAD_DIM), pr["va32"].reshape(nb, nt, H_A, HEAD_DIM),
            pr["logf"].reshape(nb, nt, H_A), pr["kb32"].reshape(nb, nt, KV_B, HEAD_DIM),
            pr["vb32"].reshape(nb, nt, KV_B, HEAD_DIM), pr["ki32"].reshape(nb, nt, D_IDX), new_g)
    return y.reshape(nb, nt, d), rows


def kernel(x_prompt, x_sample, cache_fox_k, cache_fox_v, cache_fox_logf, cache_dsa_k, cache_dsa_v, cache_idx_k,
           state_ffn_conv, page_table, w_in, b_f, g_qa, g_ka, g_qb, g_kb, g_attn, w_out, g_ffn, w_gate, w_up,
           conv_w, conv_b, w_down):
    depth = w_in.shape[0]
    t = x_prompt.shape[1]
    nb_s, nt_s = x_sample.shape[:2]
    past_len = page_table.shape[1] * PAGE_SIZE
    pos_p = jnp.arange(t, dtype=I32)
    pos_s = jnp.tile(past_len + jnp.arange(nt_s, dtype=I32), nb_s)
    tables_p = (_rope_tables(pos_p, HEAD_DIM), _rope_tables(pos_p, D_IDX))
    tables_s = (_rope_tables(pos_s, HEAD_DIM), _rope_tables(pos_s, D_IDX))
    y_p, y_s = x_prompt, x_sample
    rows_p, rows_s = [], []
    for l in range(depth):
        wts = _layer_weights(l, w_in, b_f, g_qa, g_ka, g_qb, g_kb, g_attn, w_out, g_ffn, w_gate, w_up,
                             conv_w, conv_b, w_down)
        y_p, new_p = _prompt_layer(y_p, wts, tables_p)
        pools = (cache_fox_k[l], cache_fox_v[l], cache_fox_logf[l], cache_dsa_k[l], cache_dsa_v[l], cache_idx_k[l])
        y_s, new_s = _sample_layer(y_s, wts, tables_s, pools, state_ffn_conv[l], page_table)
        rows_p.append(new_p)
        rows_s.append(new_s)
    out_p = [jnp.stack(v) for v in zip(*rows_p)]
    out_s = [jnp.stack(v) for v in zip(*rows_s)]
    return (y_p, y_s, *out_p, *out_s)
```

```python
import functools

import jax
import jax.numpy as jnp
import numpy as np
from jax import lax
from jax.experimental import pallas as pl
from jax.experimental.pallas import tpu as pltpu

F32 = jnp.float32
BF16 = jnp.bfloat16
I32 = jnp.int32

LANES = 128
SUBLANES = 8
VMEM_CAP_BYTES = 56 * 1024 * 1024

HEAD_DIM = 128
H_A = 8
H_B = 8
KV_B = 2
GROUP_B = H_B // KV_B
H_IDX = 16
D_IDX = 64
TOPK_MAX = 256
PAGE_SIZE = 128
ROPE_THETA = 10000.0
EPS = 1e-6
CONV_W = 3

NEG = -1e30
INT_MIN = -(2 ** 31)
INT_MAX = 2 ** 31 - 1
NEG_INF_KEY = int(np.int32(np.uint32(0xFF800000) ^ np.uint32(0x7FFFFFFF)))

NT_DIMS = (((1,), (1,)), ((), ()))


def _nt_dot(a, b):
    return lax.dot_general(a, b, NT_DIMS, preferred_element_type=F32)


def _dot(a, b):
    return jnp.dot(a, b, preferred_element_type=F32)


def _params(semantics, vmem_bytes):
    limit = int(min(max(vmem_bytes, 16 * 1024 * 1024), VMEM_CAP_BYTES))
    return pltpu.CompilerParams(dimension_semantics=semantics, vmem_limit_bytes=limit)


def _nbytes(shape, dtype):
    return int(np.prod(shape)) * jnp.dtype(dtype).itemsize


def _f32_to_key(x):
    b = lax.bitcast_convert_type(x, I32)
    return b ^ ((b >> 31) & INT_MAX)


def _rms_kernel(x_ref, g_ref, o_ref):
    x = x_ref[...]
    y = x * lax.rsqrt(jnp.mean(x * x, axis=-1, keepdims=True) + EPS)
    o_ref[...] = (y * g_ref[...]).astype(o_ref.dtype)


def _rms_call(x, g, tm):
    n, d = x.shape
    return pl.pallas_call(
        _rms_kernel,
        out_shape=jax.ShapeDtypeStruct((n, d), BF16),
        grid=(n // tm,),
        in_specs=[pl.BlockSpec((tm, d), lambda i: (i, 0)),
                  pl.BlockSpec((1, d), lambda i: (0, 0))],
        out_specs=pl.BlockSpec((tm, d), lambda i: (i, 0)),
        compiler_params=_params(("parallel",), 2 * (_nbytes((tm, d), F32) + _nbytes((tm, d), BF16)) + (4 << 20)),
        name="rms_norm",
    )(x, g.reshape(1, d))


def _rope_lanes(y, cos, sin, dim):
    if dim == LANES:
        rot = pltpu.roll(y, LANES // 2, 1)
    else:
        lane = lax.broadcasted_iota(I32, y.shape, 1)
        first_half = (lane % dim) < (dim // 2)
        rot = jnp.where(first_half, pltpu.roll(y, LANES - dim // 2, 1), pltpu.roll(y, dim // 2, 1))
    return y * cos + rot * sin


def _proj_kernel(*refs, has_gain, rope_dim, n32, n16):
    refs = list(refs)
    xn_ref, w_ref = refs[0], refs[1]
    pos = 2
    gain_ref = cos_ref = sin_ref = None
    if has_gain:
        gain_ref = refs[pos]
        pos += 1
    if rope_dim:
        cos_ref, sin_ref = refs[pos], refs[pos + 1]
        pos += 2
    out_refs = refs[pos:pos + n32 + n16]
    acc = _dot(xn_ref[...], w_ref[...])
    tn = acc.shape[1]
    for c in range(tn // LANES):
        cs = slice(c * LANES, (c + 1) * LANES)
        y = acc[:, cs]
        if has_gain:
            y = y * lax.rsqrt(jnp.mean(y * y, axis=-1, keepdims=True) + EPS) * gain_ref[:, cs]
        if rope_dim:
            y = _rope_lanes(y, cos_ref[...], sin_ref[...], rope_dim)
        for o in out_refs:
            o[:, cs] = y.astype(o.dtype)


def _proj_call(xn, w16, col0, ncols, tm, tn, *, gain=None, rope=None, rope_period=None, want32, want16, name):
    n, d = xn.shape
    assert col0 % tn == 0 and ncols % tn == 0 and n % tm == 0
    cb0 = col0 // tn
    in_specs = [pl.BlockSpec((tm, d), lambda i, j: (i, 0)),
                pl.BlockSpec((d, tn), lambda i, j: (0, cb0 + j))]
    args = [xn, w16]
    if gain is not None:
        in_specs.append(pl.BlockSpec((1, tn), lambda i, j: (0, j)))
        args.append(gain.reshape(1, ncols))
    rope_dim = 0
    if rope is not None:
        cos, sin, rope_dim = rope
        nper = rope_period // tm
        spec = pl.BlockSpec((tm, LANES), lambda i, j: (i % nper, 0))
        in_specs += [spec, spec]
        args += [cos, sin]
    out_shape, out_specs = [], []
    for dt, want in ((F32, want32), (BF16, want16)):
        if want:
            out_shape.append(jax.ShapeDtypeStruct((n, ncols), dt))
            out_specs.append(pl.BlockSpec((tm, tn), lambda i, j: (i, j)))
    vmem = 2 * (_nbytes((tm, d), BF16) + _nbytes((d, tn), BF16) + 2 * _nbytes((tm, tn), F32)) + 4 * _nbytes((tm, tn), F32)
    outs = pl.pallas_call(
        functools.partial(_proj_kernel, has_gain=gain is not None, rope_dim=rope_dim,
                          n32=int(want32), n16=int(want16)),
        out_shape=out_shape,
        grid=(n // tm, ncols // tn),
        in_specs=in_specs,
        out_specs=out_specs,
        compiler_params=_params(("parallel", "parallel"), vmem + (8 << 20)),
        name=name,
    )(*args)
    return outs


def _misc_kernel(xn_ref, w_ref, bf_ref, cos_ref, sin_ref, ki32_ref, ki16_ref, logf_ref, wi_ref):
    acc = _dot(xn_ref[...], w_ref[...])
    ki = _rope_lanes(acc[:, :LANES], cos_ref[...], sin_ref[...], D_IDX)[:, :D_IDX]
    ki32_ref[...] = ki
    ki16_ref[...] = ki.astype(BF16)
    z = acc[:, LANES:LANES + H_A] + bf_ref[...]
    logf_ref[...] = jnp.minimum(z, 0.0) - jnp.log1p(jnp.exp(-jnp.abs(z)))
    wi_ref[...] = acc[:, 2 * LANES:2 * LANES + H_IDX]


def _misc_call(xn, w_misc, b_f, cos, sin, tm, rope_period):
    n, d = xn.shape
    nper = rope_period // tm
    tspec = pl.BlockSpec((tm, LANES), lambda i: (i % nper, 0))
    return pl.pallas_call(
        _misc_kernel,
        out_shape=[jax.ShapeDtypeStruct((n, D_IDX), F32), jax.ShapeDtypeStruct((n, D_IDX), BF16),
                   jax.ShapeDtypeStruct((n, H_A), F32), jax.ShapeDtypeStruct((n, H_IDX), F32)],
        grid=(n // tm,),
        in_specs=[pl.BlockSpec((tm, d), lambda i: (i, 0)),
                  pl.BlockSpec((d, 3 * LANES), lambda i: (0, 0)),
                  pl.BlockSpec((1, H_A), lambda i: (0, 0)),
                  tspec, tspec],
        out_specs=[pl.BlockSpec((tm, D_IDX), lambda i: (i, 0)), pl.BlockSpec((tm, D_IDX), lambda i: (i, 0)),
                   pl.BlockSpec((tm, H_A), lambda i: (i, 0)), pl.BlockSpec((tm, H_IDX), lambda i: (i, 0))],
        compiler_params=_params(("parallel",), 32 << 20),
        name="proj_misc",
    )(xn, w_misc, b_f.reshape(1, H_A), cos, sin)


def _lane_prefix(x):
    lane = lax.broadcasted_iota(I32, x.shape, 1)
    k = 1
    while k < LANES:
        x = x + jnp.where(lane >= k, pltpu.roll(x, k, 1), 0.0)
        k *= 2
    return x


def _lane_suffix(x):
    lane = lax.broadcasted_iota(I32, x.shape, 1)
    k = 1
    while k < LANES:
        x = x + jnp.where(lane < LANES - k, pltpu.roll(x, LANES - k, 1), 0.0)
        k *= 2
    return x


def _cumsum_kernel(x_ref, o_ref):
    t = x_ref.shape[2]
    carry = jnp.zeros((x_ref.shape[1], 1), F32)
    for c in range(t // LANES):
        cs = slice(c * LANES, (c + 1) * LANES)
        y = _lane_prefix(x_ref[0, :, cs]) + carry
        o_ref[0, :, cs] = y
        carry = y[:, LANES - 1:LANES]


def _cumsum_call(logf_t):
    b, h, t = logf_t.shape
    return pl.pallas_call(
        _cumsum_kernel,
        out_shape=jax.ShapeDtypeStruct((b, h, t), F32),
        grid=(b,),
        in_specs=[pl.BlockSpec((1, h, t), lambda i: (i, 0, 0))],
        out_specs=pl.BlockSpec((1, h, t), lambda i: (i, 0, 0)),
        compiler_params=_params(("parallel",), 16 << 20),
        name="logf_cumsum",
    )(logf_t)


def _softmax_step(s, mask, h, hs, v, m_ref, l_ref, acc_ref):
    if mask is not None:
        s = jnp.where(mask, s, NEG)
    m_prev = m_ref[h]
    m_new = jnp.maximum(m_prev, jnp.max(s, axis=-1, keepdims=True))
    alpha = jnp.exp(m_prev - m_new)
    p = jnp.exp(s - m_new)
    if mask is not None:
        p = jnp.where(mask, p, 0.0)
    l_ref[h] = alpha * l_ref[h] + jnp.sum(p, axis=-1, keepdims=True)
    acc_ref[:, hs] = alpha * acc_ref[:, hs] + _dot(p.astype(BF16), v)
    m_ref[h] = m_new


def _init_softmax(m_ref, l_ref, acc_ref):
    m_ref[...] = jnp.full(m_ref.shape, NEG, F32)
    l_ref[...] = jnp.zeros(l_ref.shape, F32)
    acc_ref[...] = jnp.zeros(acc_ref.shape, F32)


def _fox_kernel(q_ref, k_ref, v_ref, cq_ref, ck_ref, o_ref, m_ref, l_ref, acc_ref, *, scale):
    i, j = pl.program_id(1), pl.program_id(2)
    tq, tk = q_ref.shape[0], k_ref.shape[0]

    @pl.when(j == 0)
    def _():
        _init_softmax(m_ref, l_ref, acc_ref)

    def step(diag):
        mask = None
        if diag:
            mask = lax.broadcasted_iota(I32, (tq, tk), 0) >= lax.broadcasted_iota(I32, (tq, tk), 1)
        for h in range(H_A):
            hs = slice(h * HEAD_DIM, (h + 1) * HEAD_DIM)
            s = _nt_dot(q_ref[:, hs], k_ref[:, hs]) * scale
            s = s + cq_ref[:, h:h + 1] - ck_ref[0, h:h + 1, :]
            _softmax_step(s, mask, h, hs, v_ref[:, hs], m_ref, l_ref, acc_ref)

    pl.when(j < i)(functools.partial(step, False))

    @pl.when(j == i)
    def _():
        step(True)
        for h in range(H_A):
            hs = slice(h * HEAD_DIM, (h + 1) * HEAD_DIM)
            o_ref[:, hs] = (acc_ref[:, hs] / l_ref[h]).astype(o_ref.dtype)


def _fox_call(q16, k16, v16, c_rows, c_lanes, nb, t, tq):
    nq = t // tq
    width = H_A * HEAD_DIM
    qmap = lambda b, i, j: (b * nq + i, 0)
    kmap = lambda b, i, j: (b * nq + jnp.minimum(j, i), 0)
    vmem = 2 * 4 * _nbytes((tq, width), BF16) + _nbytes((tq, width), F32) + 2 * H_A * _nbytes((tq, LANES), F32)
    return pl.pallas_call(
        functools.partial(_fox_kernel, scale=HEAD_DIM ** -0.5),
        out_shape=jax.ShapeDtypeStruct((nb * t, width), BF16),
        grid=(nb, nq, nq),
        in_specs=[pl.BlockSpec((tq, width), qmap),
                  pl.BlockSpec((tq, width), kmap),
                  pl.BlockSpec((tq, width), kmap),
                  pl.BlockSpec((tq, H_A), qmap),
                  pl.BlockSpec((1, H_A, tq), lambda b, i, j: (b, 0, jnp.minimum(j, i)))],
        out_specs=pl.BlockSpec((tq, width), qmap),
        scratch_shapes=[pltpu.VMEM((H_A, tq, 1), F32), pltpu.VMEM((H_A, tq, 1), F32),
                        pltpu.VMEM((tq, width), F32)],
        compiler_params=_params(("parallel", "parallel", "arbitrary"), vmem + (16 << 20)),
        name="fox_prompt_attention",
    )(q16, k16, v16, c_rows, c_lanes)


def _count_where(key_ref, n_chunks, chunk, rows, pred):
    def body(c, acc):
        start = pl.multiple_of(c * chunk, chunk)
        blk = key_ref[:, pl.ds(start, chunk)]
        for s in range(chunk // LANES):
            col = start + s * LANES + lax.broadcasted_iota(I32, (rows, LANES), 1)
            acc = acc + jnp.where(pred(blk[:, s * LANES:(s + 1) * LANES], col), 1.0, 0.0)
        return acc
    acc = lax.fori_loop(0, n_chunks, body, jnp.zeros((rows, LANES), F32))
    return jnp.sum(acc, axis=-1, keepdims=True)


def _select_threshold(key_ref, n_chunks, chunk, rows, n_sel, thr_ref, jcut_ref):
    nf = float(n_sel)

    def count_ge(cand):
        return _count_where(key_ref, n_chunks, chunk, rows, lambda k, col: k >= cand)

    base = jnp.where(count_ge(jnp.zeros((rows, 1), I32)) >= nf, 0, INT_MIN).astype(I32)

    def bit_body(b, base):
        cand = base | jnp.left_shift(jnp.int32(1), 30 - b)
        return jnp.where(count_ge(cand) >= nf, cand, base)

    base = lax.fori_loop(0, 31, bit_body, base)
    n_ge = count_ge(base)
    n_gt = count_ge(base + 1)
    need = jnp.logical_and(n_ge > nf, base != NEG_INF_KEY)
    thr_ref[...] = base.reshape(thr_ref.shape)
    jcut_ref[...] = jnp.full(jcut_ref.shape, INT_MAX, I32)

    @pl.when(jnp.max(jnp.where(need, 1.0, 0.0)) > 0.0)
    def _():
        want = nf - n_gt

        def count_eq_below(x):
            return _count_where(key_ref, n_chunks, chunk, rows,
                                lambda k, col: jnp.logical_and(k == base, col < x))

        def col_body(b, x):
            cand = x | jnp.left_shift(jnp.int32(1), 14 - b)
            return jnp.where(count_eq_below(cand) < want, cand, x)

        x = lax.fori_loop(0, 15, col_body, jnp.zeros((rows, 1), I32))
        jcut_ref[...] = jnp.where(need, x, INT_MAX).reshape(jcut_ref.shape)


def _canon_zero(x):
    return jnp.where(x == 0.0, 0.0, x)


def _idx_kernel(qi_ref, wi_ref, ki_ref, sc_ref, thr_ref, jcut_ref, key_ref, *, n_sel, idx_scale):
    i, j = pl.program_id(1), pl.program_id(2)
    tq, tk = qi_ref.shape[0], ki_ref.shape[0]

    def scores(diag):
        acc = jnp.zeros((tq, tk), F32)
        for h in range(H_IDX):
            d = _nt_dot(qi_ref[:, h * D_IDX:(h + 1) * D_IDX], ki_ref[...])
            acc = acc + wi_ref[:, h:h + 1] * jnp.maximum(d, 0.0)
        sc = _canon_zero(acc * idx_scale)
        if diag:
            causal = lax.broadcasted_iota(I32, (tq, tk), 0) >= lax.broadcasted_iota(I32, (tq, tk), 1)
            sc = jnp.where(causal, sc, -jnp.inf)
        key = _f32_to_key(sc)
        sc_ref[...] = key
        key_ref[:, pl.ds(pl.multiple_of(j * tk, tk), tk)] = key

    pl.when(j < i)(functools.partial(scores, False))

    @pl.when(j == i)
    def _():
        scores(True)
        _select_threshold(key_ref, i + 1, tk, tq, n_sel, thr_ref, jcut_ref)


def _idx_call(qi16, wi, ki16, nb, t, tq, n_sel):
    nq = t // tq
    qmap = lambda b, i, j: (b * nq + i, 0)
    vmem = _nbytes((tq, t), I32) + 2 * (_nbytes((tq, H_IDX * D_IDX), BF16) + 2 * _nbytes((tq, tq), F32))
    return pl.pallas_call(
        functools.partial(_idx_kernel, n_sel=n_sel, idx_scale=(H_IDX * D_IDX) ** -0.5),
        out_shape=[jax.ShapeDtypeStruct((nb * t, t), I32),
                   jax.ShapeDtypeStruct((nb * t, 1), I32),
                   jax.ShapeDtypeStruct((nb * t, 1), I32)],
        grid=(nb, nq, nq),
        in_specs=[pl.BlockSpec((tq, H_IDX * D_IDX), qmap),
                  pl.BlockSpec((tq, H_IDX), qmap),
                  pl.BlockSpec((tq, D_IDX), lambda b, i, j: (b * nq + jnp.minimum(j, i), 0))],
        out_specs=[pl.BlockSpec((tq, tq), lambda b, i, j: (b * nq + i, jnp.minimum(j, i))),
                   pl.BlockSpec((tq, 1), qmap),
                   pl.BlockSpec((tq, 1), qmap)],
        scratch_shapes=[pltpu.VMEM((tq, t), I32)],
        compiler_params=_params(("parallel", "parallel", "arbitrary"), vmem + (16 << 20)),
        name="dsa_prompt_indexer",
    )(qi16, wi, ki16)


def _selected(key, thr, jcut, col):
    return jnp.logical_or(key > thr, jnp.logical_and(key == thr, col <= jcut))


def _dsa_kernel(q_ref, k_ref, v_ref, sc_ref, thr_ref, jcut_ref, o_ref, m_ref, l_ref, acc_ref, *, scale):
    i, j = pl.program_id(1), pl.program_id(2)
    tq, tk = q_ref.shape[0], k_ref.shape[0]

    @pl.when(j == 0)
    def _():
        _init_softmax(m_ref, l_ref, acc_ref)

    @pl.when(j <= i)
    def _():
        row = i * tq + lax.broadcasted_iota(I32, (tq, tk), 0)
        col = j * tk + lax.broadcasted_iota(I32, (tq, tk), 1)
        mask = jnp.logical_and(row >= col, _selected(sc_ref[...], thr_ref[...], jcut_ref[...], col))
        for h in range(H_B):
            hs = slice(h * HEAD_DIM, (h + 1) * HEAD_DIM)
            g = h // GROUP_B
            gs = slice(g * HEAD_DIM, (g + 1) * HEAD_DIM)
            s = _nt_dot(q_ref[:, hs], k_ref[:, gs]) * scale
            _softmax_step(s, mask, h, hs, v_ref[:, gs], m_ref, l_ref, acc_ref)

    @pl.when(j == i)
    def _():
        for h in range(H_B):
            hs = slice(h * HEAD_DIM, (h + 1) * HEAD_DIM)
            o_ref[:, hs] = (acc_ref[:, hs] / l_ref[h]).astype(o_ref.dtype)


def _dsa_call(q16, k16, v16, sc, thr, jcut, nb, t, tq):
    nq = t // tq
    qw, kw = H_B * HEAD_DIM, KV_B * HEAD_DIM
    qmap = lambda b, i, j: (b * nq + i, 0)
    kmap = lambda b, i, j: (b * nq + jnp.minimum(j, i), 0)
    vmem = 2 * (2 * _nbytes((tq, qw), BF16) + 2 * _nbytes((tq, kw), BF16) + _nbytes((tq, tq), F32)) \
        + _nbytes((tq, qw), F32) + 2 * H_B * _nbytes((tq, LANES), F32)
    return pl.pallas_call(
        functools.partial(_dsa_kernel, scale=HEAD_DIM ** -0.5),
        out_shape=jax.ShapeDtypeStruct((nb * t, qw), BF16),
        grid=(nb, nq, nq),
        in_specs=[pl.BlockSpec((tq, qw), qmap),
                  pl.BlockSpec((tq, kw), kmap),
                  pl.BlockSpec((tq, kw), kmap),
                  pl.BlockSpec((tq, tq), lambda b, i, j: (b * nq + i, jnp.minimum(j, i))),
                  pl.BlockSpec((tq, 1), qmap),
                  pl.BlockSpec((tq, 1), qmap)],
        out_specs=pl.BlockSpec((tq, qw), qmap),
        scratch_shapes=[pltpu.VMEM((H_B, tq, 1), F32), pltpu.VMEM((H_B, tq, 1), F32),
                        pltpu.VMEM((tq, qw), F32)],
        compiler_params=_params(("parallel", "parallel", "arbitrary"), vmem + (16 << 20)),
        name="dsa_prompt_attention",
    )(q16, k16, v16, sc, thr, jcut)


FOX_PAGES_PER_STEP = 8
DSA_PAGES_PER_STEP = 16


def _rows_from_heads(x, reps):
    return jnp.concatenate([jnp.broadcast_to(x[h:h + 1, :], (reps, x.shape[1])) for h in range(x.shape[0])], axis=0)


def _decode_softmax_step(s_tiles, mask_tiles, v_tiles, m_ref, l_ref, acc_ref):
    s = jnp.concatenate(s_tiles, axis=1)
    mask = None
    if mask_tiles is not None:
        mask = jnp.concatenate(mask_tiles, axis=1)
        s = jnp.where(mask, s, NEG)
    m_prev = m_ref[...]
    m_new = jnp.maximum(m_prev, jnp.max(s, axis=-1, keepdims=True))
    alpha = jnp.exp(m_prev - m_new)
    p = jnp.exp(s - m_new)
    if mask is not None:
        p = jnp.where(mask, p, 0.0)
    l_ref[...] = alpha * l_ref[...] + jnp.sum(p, axis=-1, keepdims=True)
    p16 = p.astype(BF16)
    pv = _dot(p16[:, :PAGE_SIZE], v_tiles[0])
    for t in range(1, len(v_tiles)):
        pv = pv + _dot(p16[:, t * PAGE_SIZE:(t + 1) * PAGE_SIZE], v_tiles[t])
    acc_ref[...] = alpha * acc_ref[...] + pv
    m_ref[...] = m_new


def _fox_decode_kernel(pt_ref, q_ref, lfq_ref, knew_ref, vnew_ref, lfnew_ref, *rest, n_tok, scale):
    del pt_ref
    npg = FOX_PAGES_PER_STEP
    k_refs, v_refs, lf_refs = rest[:npg], rest[npg:2 * npg], rest[2 * npg:3 * npg]
    o_ref, m_ref, l_ref, acc_ref, carry_ref, rowc_ref = rest[3 * npg:]
    s_id = pl.program_id(1)
    rows = H_A * n_tok

    def logits(k16, key_bias):
        s = _nt_dot(q_ref[0], k16) * scale
        return s + rowc_ref[...] + _rows_from_heads(key_bias, n_tok)

    @pl.when(s_id == 0)
    def _():
        _init_softmax(m_ref, l_ref, acc_ref)
        lf = lfq_ref[...]
        tok = lax.broadcasted_iota(I32, lf.shape, 0)
        after = jnp.zeros(lf.shape, F32)
        for jj in range(1, n_tok):
            after = after + jnp.where(tok < jj, lf[jj:jj + 1, :], 0.0)
        rowc_ref[...] = jnp.concatenate([-after[:, h:h + 1] for h in range(H_A)], axis=0)
        lfn = lfnew_ref[0]
        incl = _lane_suffix(lfn)
        tokq = lax.broadcasted_iota(I32, (rows, LANES), 0) % n_tok
        mask = lax.broadcasted_iota(I32, (rows, LANES), 1) <= tokq
        _decode_softmax_step([logits(knew_ref[0], incl - lfn)], [mask], [vnew_ref[0]], m_ref, l_ref, acc_ref)
        carry_ref[...] = incl[:, 0:1]

    @pl.when(s_id > 0)
    def _():
        carry = carry_ref[...]
        s_tiles = [None] * npg
        for p in reversed(range(npg)):
            lf = lf_refs[p][0]
            incl = _lane_suffix(lf)
            s_tiles[p] = logits(k_refs[p][0].astype(BF16), incl - lf + carry)
            carry = carry + incl[:, 0:1]
        carry_ref[...] = carry
        _decode_softmax_step(s_tiles, None, [v_refs[p][0].astype(BF16) for p in range(npg)],
                             m_ref, l_ref, acc_ref)

    @pl.when(s_id == pl.num_programs(1) - 1)
    def _():
        for h in range(H_A):
            rs = slice(h * n_tok, (h + 1) * n_tok)
            hs = slice(h * HEAD_DIM, (h + 1) * HEAD_DIM)
            o_ref[:, hs] = (acc_ref[rs, hs] / l_ref[rs, :]).astype(o_ref.dtype)


def _page_specs(block, npg, group_of_step):
    specs = []
    for p in range(npg):
        def imap(b, s, pt, p=p):
            return (pt[b, group_of_step(s) * npg + p],) + (0,) * (len(block) - 1)
        specs.append(pl.BlockSpec(block, imap))
    return specs


def _fox_decode_call(page_table, qbd, lf_rows, knew, vnew, lfnew, k_pool, v_pool, lf_pool, n_tok):
    nb, n_pages = page_table.shape
    npg = FOX_PAGES_PER_STEP
    n_groups = n_pages // npg
    assert n_groups * npg == n_pages
    rows, width = H_A * n_tok, H_A * HEAD_DIM
    bmap = lambda b, s, pt: (b, 0, 0)
    mirrored = lambda s: n_groups - 1 - jnp.maximum(s - 1, 0)
    in_specs = [pl.BlockSpec((1, rows, width), bmap),
                pl.BlockSpec((n_tok, H_A), lambda b, s, pt: (b, 0)),
                pl.BlockSpec((1, PAGE_SIZE, width), bmap),
                pl.BlockSpec((1, PAGE_SIZE, width), bmap),
                pl.BlockSpec((1, H_A, PAGE_SIZE), bmap)]
    in_specs += _page_specs((1, PAGE_SIZE, width), npg, mirrored)
    in_specs += _page_specs((1, PAGE_SIZE, width), npg, mirrored)
    in_specs += _page_specs((1, H_A, PAGE_SIZE), npg, mirrored)
    vmem = 2 * 2 * npg * _nbytes((PAGE_SIZE, width), F32) + 4 * _nbytes((rows, width), F32) \
        + 3 * npg * _nbytes((PAGE_SIZE, width), BF16)
    return pl.pallas_call(
        functools.partial(_fox_decode_kernel, n_tok=n_tok, scale=HEAD_DIM ** -0.5),
        out_shape=jax.ShapeDtypeStruct((nb * n_tok, width), BF16),
        grid_spec=pltpu.PrefetchScalarGridSpec(
            num_scalar_prefetch=1,
            grid=(nb, n_groups + 1),
            in_specs=in_specs,
            out_specs=pl.BlockSpec((n_tok, width), lambda b, s, pt: (b, 0)),
            scratch_shapes=[pltpu.VMEM((rows, 1), F32), pltpu.VMEM((rows, 1), F32),
                            pltpu.VMEM((rows, width), F32), pltpu.VMEM((H_A, 1), F32),
                            pltpu.VMEM((rows, 1), F32)]),
        compiler_params=_params(("parallel", "arbitrary"), vmem + (16 << 20)),
        name="fox_decode_attention",
    )(page_table, qbd, lf_rows, knew, vnew, lfnew, *([k_pool] * npg), *([v_pool] * npg), *([lf_pool] * npg))


def _idx_decode_kernel(pt_ref, qi_ref, wi_ref, kinew_ref, *rest, n_tok, n_sel, idx_scale):
    del pt_ref
    npg = DSA_PAGES_PER_STEP
    ki_refs = rest[:npg]
    sc_ref, thr_ref, jcut_ref, key_ref = rest[npg:]
    s_id = pl.program_id(1)
    last = pl.num_programs(1) - 1
    width = npg * PAGE_SIZE

    def page_scores(ki16):
        d = jnp.maximum(_nt_dot(qi_ref[0], ki16), 0.0) * wi_ref[0]
        acc = jnp.zeros((n_tok, PAGE_SIZE), F32)
        for h in range(H_IDX):
            acc = acc + d[h * n_tok:(h + 1) * n_tok, :]
        return _canon_zero(acc * idx_scale)

    def put(p, key):
        sc_ref[0, :, p * PAGE_SIZE:(p + 1) * PAGE_SIZE] = key
        key_ref[:, pl.ds(pl.multiple_of(s_id * width + p * PAGE_SIZE, PAGE_SIZE), PAGE_SIZE)] = key

    @pl.when(s_id < last)
    def _():
        for p in range(npg):
            put(p, _f32_to_key(page_scores(ki_refs[p][0].astype(BF16))))

    @pl.when(s_id == last)
    def _():
        sc = page_scores(kinew_ref[0])
        causal = lax.broadcasted_iota(I32, sc.shape, 1) <= lax.broadcasted_iota(I32, sc.shape, 0)
        put(0, _f32_to_key(jnp.where(causal, sc, -jnp.inf)))
        for p in range(1, npg):
            put(p, jnp.full((n_tok, PAGE_SIZE), NEG_INF_KEY, I32))
        _select_threshold(key_ref, s_id + 1, width, n_tok, n_sel, thr_ref, jcut_ref)


def _idx_decode_call(page_table, qi_rows, wi_rows, kinew, ki_pool, n_tok, n_sel):
    nb, n_pages = page_table.shape
    npg = DSA_PAGES_PER_STEP
    n_groups = n_pages // npg
    assert n_groups * npg == n_pages
    width = npg * PAGE_SIZE
    n_cols = (n_groups + 1) * width
    bmap = lambda b, s, pt: (b, 0, 0)
    forward = lambda s: jnp.minimum(s, n_groups - 1)
    in_specs = [pl.BlockSpec((1, H_IDX * n_tok, D_IDX), bmap),
                pl.BlockSpec((1, H_IDX * n_tok, 1), bmap),
                pl.BlockSpec((1, PAGE_SIZE, D_IDX), bmap)]
    in_specs += _page_specs((1, PAGE_SIZE, D_IDX), npg, forward)
    return pl.pallas_call(
        functools.partial(_idx_decode_kernel, n_tok=n_tok, n_sel=n_sel, idx_scale=(H_IDX * D_IDX) ** -0.5),
        out_shape=[jax.ShapeDtypeStruct((nb, n_tok, n_cols), I32),
                   jax.ShapeDtypeStruct((nb, n_tok, 1), I32),
                   jax.ShapeDtypeStruct((nb, n_tok, 1), I32)],
        grid_spec=pltpu.PrefetchScalarGridSpec(
            num_scalar_prefetch=1,
            grid=(nb, n_groups + 1),
            in_specs=in_specs,
            out_specs=[pl.BlockSpec((1, n_tok, width), lambda b, s, pt: (b, 0, s)),
                       pl.BlockSpec((1, n_tok, 1), bmap),
                       pl.BlockSpec((1, n_tok, 1), bmap)],
            scratch_shapes=[pltpu.VMEM((n_tok, n_cols), I32)]),
        compiler_params=_params(("parallel", "arbitrary"), 24 << 20),
        name="dsa_decode_indexer",
    )(page_table, qi_rows, wi_rows, kinew, *([ki_pool] * npg))


def _dsa_decode_kernel(pt_ref, q_ref, knew_ref, vnew_ref, sc_ref, thr_ref, jcut_ref, *rest, n_tok, scale):
    del pt_ref
    npg = DSA_PAGES_PER_STEP
    k_refs, v_refs = rest[:npg], rest[npg:2 * npg]
    o_ref, m_ref, l_ref, acc_ref = rest[2 * npg:]
    s_id = pl.program_id(1)
    last = pl.num_programs(1) - 1
    width = npg * PAGE_SIZE

    @pl.when(s_id == 0)
    def _():
        _init_softmax(m_ref, l_ref, acc_ref)

    def tile_mask(p, extra):
        col = s_id * width + p * PAGE_SIZE + lax.broadcasted_iota(I32, (n_tok, PAGE_SIZE), 1)
        sel = _selected(sc_ref[0, :, p * PAGE_SIZE:(p + 1) * PAGE_SIZE], thr_ref[0], jcut_ref[0], col)
        if extra is not None:
            sel = jnp.logical_and(sel, extra)
        return jnp.concatenate([jnp.where(sel, 1.0, 0.0)] * H_B, axis=0) > 0.5

    @pl.when(s_id < last)
    def _():
        _decode_softmax_step([_nt_dot(q_ref[0], k_refs[p][0].astype(BF16)) * scale for p in range(npg)],
                             [tile_mask(p, None) for p in range(npg)],
                             [v_refs[p][0].astype(BF16) for p in range(npg)], m_ref, l_ref, acc_ref)

    @pl.when(s_id == last)
    def _():
        causal = lax.broadcasted_iota(I32, (n_tok, PAGE_SIZE), 1) <= lax.broadcasted_iota(I32, (n_tok, PAGE_SIZE), 0)
        _decode_softmax_step([_nt_dot(q_ref[0], knew_ref[0]) * scale], [tile_mask(0, causal)], [vnew_ref[0]],
                             m_ref, l_ref, acc_ref)
        for h in range(H_B):
            rs = slice(h * n_tok, (h + 1) * n_tok)
            g = h // GROUP_B
            o_ref[:, h * HEAD_DIM:(h + 1) * HEAD_DIM] = (
                acc_ref[rs, g * HEAD_DIM:(g + 1) * HEAD_DIM] / l_ref[rs, :]).astype(o_ref.dtype)


def _dsa_decode_call(page_table, qbd, knew, vnew, sc, thr, jcut, k_pool, v_pool, n_tok):
    nb, n_pages = page_table.shape
    npg = DSA_PAGES_PER_STEP
    n_groups = n_pages // npg
    assert n_groups * npg == n_pages
    width = npg * PAGE_SIZE
    rows, kw = H_B * n_tok, KV_B * HEAD_DIM
    bmap = lambda b, s, pt: (b, 0, 0)
    forward = lambda s: jnp.minimum(s, n_groups - 1)
    in_specs = [pl.BlockSpec((1, rows, kw), bmap),
                pl.BlockSpec((1, PAGE_SIZE, kw), bmap),
                pl.BlockSpec((1, PAGE_SIZE, kw), bmap),
                pl.BlockSpec((1, n_tok, width), lambda b, s, pt: (b, 0, s)),
                pl.BlockSpec((1, n_tok, 1), bmap),
                pl.BlockSpec((1, n_tok, 1), bmap)]
    in_specs += _page_specs((1, PAGE_SIZE, kw), npg, forward)
    in_specs += _page_specs((1, PAGE_SIZE, kw), npg, forward)
    return pl.pallas_call(
        functools.partial(_dsa_decode_kernel, n_tok=n_tok, scale=HEAD_DIM ** -0.5),
        out_shape=jax.ShapeDtypeStruct((nb * n_tok, H_B * HEAD_DIM), BF16),
        grid_spec=pltpu.PrefetchScalarGridSpec(
            num_scalar_prefetch=1,
            grid=(nb, n_groups + 1),
            in_specs=in_specs,
            out_specs=pl.BlockSpec((n_tok, H_B * HEAD_DIM), lambda b, s, pt: (b, 0)),
            scratch_shapes=[pltpu.VMEM((rows, 1), F32), pltpu.VMEM((rows, 1), F32),
                            pltpu.VMEM((rows, kw), F32)]),
        compiler_params=_params(("parallel", "arbitrary"), 24 << 20),
        name="dsa_decode_attention",
    )(page_table, qbd, knew, vnew, sc, thr, jcut, *([k_pool] * npg), *([v_pool] * npg))


def _outproj_kernel(oa_ref, ob_ref, w_ref, x_ref, h_ref):
    ka = oa_ref.shape[1]
    h_ref[...] = x_ref[...] + _dot(oa_ref[...], w_ref[:ka, :]) + _dot(ob_ref[...], w_ref[ka:, :])


def _outproj_call(oa, ob, w16, x, tm, tn):
    n, d = x.shape
    ka, kb = oa.shape[1], ob.shape[1]
    vmem = 2 * (_nbytes((tm, ka + kb), BF16) + _nbytes((ka + kb, tn), BF16) + 2 * _nbytes((tm, tn), F32))
    return pl.pallas_call(
        _outproj_kernel,
        out_shape=jax.ShapeDtypeStruct((n, d), F32),
        grid=(n // tm, d // tn),
        in_specs=[pl.BlockSpec((tm, ka), lambda i, j: (i, 0)),
                  pl.BlockSpec((tm, kb), lambda i, j: (i, 0)),
                  pl.BlockSpec((ka + kb, tn), lambda i, j: (0, j)),
                  pl.BlockSpec((tm, tn), lambda i, j: (i, j))],
        out_specs=pl.BlockSpec((tm, tn), lambda i, j: (i, j)),
        compiler_params=_params(("parallel", "parallel"), vmem + (8 << 20)),
        name="out_proj",
    )(oa, ob, w16, x)


HALO = SUBLANES


def _shift_rows(g, k, row, fill):
    out = pltpu.roll(g, k, 0)
    for r in range(k):
        out = jnp.where(row == r, fill(r), out)
    return out


def _ffn_kernel(*refs, seq_rows, has_state):
    if has_state:
        (h_ref, gn_ref, wg_ref, wu_ref, cw_ref, cb_ref, wd_ref, s1_ref, s2_ref,
         y_ref, gl_ref, hn_ref, acc_ref) = refs
    else:
        (h_ref, gn_ref, wg_ref, wu_ref, cw_ref, cb_ref, wd_ref,
         y_ref, gl_ref, hn_ref, acc_ref, halo_ref) = refs
    i, j = pl.program_id(0), pl.program_id(1)
    tm, tf = h_ref.shape[0], wg_ref.shape[1]

    @pl.when(j == 0)
    def _():
        x = h_ref[...]
        y = x * lax.rsqrt(jnp.mean(x * x, axis=-1, keepdims=True) + EPS)
        hn_ref[...] = (y * gn_ref[...]).astype(BF16)
        acc_ref[...] = jnp.zeros(acc_ref.shape, F32)

    hn = hn_ref[...]
    g = _dot(hn, wg_ref[...])
    u = _dot(hn, wu_ref[...])
    row = lax.broadcasted_iota(I32, (tm, tf), 0)
    if has_state:
        tpos = row % seq_rows
        gm1 = jnp.where(tpos >= 1, pltpu.roll(g, 1, 0), s1_ref[...])
        gm2 = jnp.where(tpos >= 2, pltpu.roll(g, 2, 0), s2_ref[...])
        gl_ref[...] = g
    else:
        tiles_per_seq = seq_rows // tm

        @pl.when(i % tiles_per_seq == 0)
        def _():
            halo_ref[j] = jnp.zeros((HALO, tf), F32)

        prev = halo_ref[j]
        gm1 = _shift_rows(g, 1, row, lambda r: prev[HALO - 1 + r:HALO + r, :])
        gm2 = _shift_rows(g, 2, row, lambda r: prev[HALO - 2 + r:HALO - 1 + r, :])
        halo_ref[j] = g[tm - HALO:, :]
        gl_ref[0] = g[tm - HALO:, :]
    gc = cb_ref[...] + cw_ref[0:1, :] * gm2 + cw_ref[1:2, :] * gm1 + cw_ref[2:3, :] * g
    act = gc * (1.0 / (1.0 + jnp.exp(-gc))) * u
    acc_ref[...] += _dot(act.astype(BF16), wd_ref[...])

    @pl.when(j == pl.num_programs(1) - 1)
    def _():
        y_ref[...] = h_ref[...] + acc_ref[...]


def _ffn_call(h, g_ffn, wg16, wu16, cw, cb, wd16, tm, tf, seq_rows, state=None):
    n, d = h.shape
    dff = wg16.shape[1]
    nj = dff // tf
    has_state = state is not None
    in_specs = [pl.BlockSpec((tm, d), lambda i, j: (i, 0)),
                pl.BlockSpec((1, d), lambda i, j: (0, 0)),
                pl.BlockSpec((d, tf), lambda i, j: (0, j)),
                pl.BlockSpec((d, tf), lambda i, j: (0, j)),
                pl.BlockSpec((CONV_W, tf), lambda i, j: (0, j)),
                pl.BlockSpec((1, tf), lambda i, j: (0, j)),
                pl.BlockSpec((tf, d), lambda i, j: (j, 0))]
    args = [h, g_ffn.reshape(1, d), wg16, wu16, cw, cb.reshape(1, dff), wd16]
    scratch = [pltpu.VMEM((tm, d), BF16), pltpu.VMEM((tm, d), F32)]
    if has_state:
        in_specs += [pl.BlockSpec((tm, tf), lambda i, j: (i, j))] * 2
        args += list(state)
        gl_shape = jax.ShapeDtypeStruct((n, dff), F32)
        gl_spec = pl.BlockSpec((tm, tf), lambda i, j: (i, j))
    else:
        assert seq_rows % tm == 0
        gl_shape = jax.ShapeDtypeStruct((n // tm, HALO, dff), F32)
        gl_spec = pl.BlockSpec((1, HALO, tf), lambda i, j: (i, 0, j))
        scratch.append(pltpu.VMEM((nj, HALO, tf), F32))
    vmem = 2 * (2 * _nbytes((tm, d), F32) + 3 * _nbytes((d, tf), BF16)) + _nbytes((tm, d), BF16) \
        + _nbytes((tm, d), F32) + 6 * _nbytes((tm, tf), F32)
    return pl.pallas_call(
        functools.partial(_ffn_kernel, seq_rows=seq_rows, has_state=has_state),
        out_shape=[jax.ShapeDtypeStruct((n, d), F32), gl_shape],
        grid=(n // tm, nj),
        in_specs=in_specs,
        out_specs=[pl.BlockSpec((tm, d), lambda i, j: (i, 0)), gl_spec],
        scratch_shapes=scratch,
        compiler_params=_params(("arbitrary", "arbitrary"), vmem + (8 << 20)),
        name="conv_ffn_state" if has_state else "conv_ffn",
    )(*args)


def _rope_tables(pos, dim):
    half = dim // 2
    inv = ROPE_THETA ** (-jnp.arange(half, dtype=F32) / half)
    ang = pos.astype(F32)[:, None] * inv[None, :]
    cos = jnp.tile(jnp.concatenate([jnp.cos(ang), jnp.cos(ang)], axis=-1), (1, LANES // dim))
    sin = jnp.tile(jnp.concatenate([-jnp.sin(ang), jnp.sin(ang)], axis=-1), (1, LANES // dim))
    return cos, sin


def _pad_to(x, size, axis):
    pad = [(0, 0)] * x.ndim
    pad[axis] = (0, size - x.shape[axis])
    return jnp.pad(x, pad)


def _project(x2d, wts, tables, tm, rope_period):
    xn = _rms_call(x2d, wts["g_attn"], tm)
    w16 = wts["w_main"]
    (cos_h, sin_h), (cos_i, sin_i) = tables
    hw = H_A * HEAD_DIM
    kw = KV_B * HEAD_DIM
    call = functools.partial(_proj_call, xn, w16, tm=tm)
    (qa16,) = call(0, hw, tn=512, gain=jnp.tile(wts["g_qa"], H_A), want32=False, want16=True, name="proj_qa")
    ka32, ka16 = call(hw, hw, tn=512, gain=jnp.tile(wts["g_ka"], H_A), want32=True, want16=True, name="proj_ka")
    va32, va16 = call(2 * hw, hw, tn=512, want32=True, want16=True, name="proj_va")
    (qb16,) = call(3 * hw, hw, tn=512, gain=jnp.tile(wts["g_qb"], H_B), rope=(cos_h, sin_h, HEAD_DIM),
                   rope_period=rope_period, want32=False, want16=True, name="proj_qb")
    kb32, kb16 = call(4 * hw, kw, tn=kw, gain=jnp.tile(wts["g_kb"], KV_B), rope=(cos_h, sin_h, HEAD_DIM),
                      rope_period=rope_period, want32=True, want16=True, name="proj_kb")
    vb32, vb16 = call(4 * hw + kw, kw, tn=kw, want32=True, want16=True, name="proj_vb")
    (qi16,) = call(4 * hw + 2 * kw, H_IDX * D_IDX, tn=512, rope=(cos_i, sin_i, D_IDX),
                   rope_period=rope_period, want32=False, want16=True, name="proj_qi")
    ki32, ki16, logf, wi = _misc_call(xn, wts["w_misc"], wts["b_f"], cos_i, sin_i, tm, rope_period)
    return dict(qa16=qa16, ka32=ka32, ka16=ka16, va32=va32, va16=va16, qb16=qb16, kb32=kb32, kb16=kb16,
                vb32=vb32, vb16=vb16, qi16=qi16, ki32=ki32, ki16=ki16, logf=logf, wi=wi)


def _layer_weights(l, w_in, b_f, g_qa, g_ka, g_qb, g_kb, g_attn, w_out, g_ffn, w_gate, w_up, conv_w, conv_b, w_down):
    hw, kw = H_A * HEAD_DIM, KV_B * HEAD_DIM
    splits = (hw, hw, hw, H_A, H_B * HEAD_DIM, kw, kw, H_IDX * D_IDX, D_IDX, H_IDX)
    pts = [0] + [int(v) for v in np.cumsum(splits)]
    col = lambda k: w_in[l][:, pts[k]:pts[k + 1]]
    w_main = jnp.concatenate([col(0), col(1), col(2), col(4), col(5), col(6), col(7)], axis=1).astype(BF16)
    w_misc = jnp.concatenate([_pad_to(col(8), LANES, 1), _pad_to(col(3), LANES, 1), _pad_to(col(9), LANES, 1)],
                             axis=1).astype(BF16)
    dff = w_gate.shape[2]
    dffp = -(-dff // 512) * 512
    return dict(
        w_main=w_main, w_misc=w_misc, b_f=b_f[l], g_qa=g_qa[l], g_ka=g_ka[l], g_qb=g_qb[l], g_kb=g_kb[l],
        g_attn=g_attn[l], w_out=w_out[l].astype(BF16), g_ffn=g_ffn[l],
        w_gate=_pad_to(w_gate[l], dffp, 1).astype(BF16), w_up=_pad_to(w_up[l], dffp, 1).astype(BF16),
        conv_w=_pad_to(conv_w[l], dffp, 1), conv_b=_pad_to(conv_b[l], dffp, 0),
        w_down=_pad_to(w_down[l], dffp, 0).astype(BF16), dff=dff)


def _prompt_layer(x, wts, tables):
    nb, t, d = x.shape
    x2d = x.reshape(nb * t, d)
    pr = _project(x2d, wts, tables, tm=512, rope_period=t)
    logf_t = jnp.swapaxes(pr["logf"].reshape(nb, t, H_A), 1, 2)
    c_lanes = _cumsum_call(logf_t)
    c_rows = jnp.swapaxes(c_lanes, 1, 2).reshape(nb * t, H_A)
    oa = _fox_call(pr["qa16"], pr["ka16"], pr["va16"], c_rows, c_lanes, nb, t, tq=512)
    n_sel = min(TOPK_MAX, t // 4)
    sc, thr, jcut = _idx_call(pr["qi16"], pr["wi"], pr["ki16"], nb, t, tq=256, n_sel=n_sel)
    ob = _dsa_call(pr["qb16"], pr["kb16"], pr["vb16"], sc, thr, jcut, nb, t, tq=256)
    h = _outproj_call(oa, ob, wts["w_out"], x2d, tm=512, tn=512)
    y, glast = _ffn_call(h, wts["g_ffn"], wts["w_gate"], wts["w_up"], wts["conv_w"], wts["conv_b"], wts["w_down"],
                         tm=512, tf=512, seq_rows=t)
    dff = wts["dff"]
    new_g = glast.reshape(nb, t // 512, HALO, -1)[:, -1, HALO - (CONV_W - 1):, :dff]
    rows = (pr["ka32"].reshape(nb, t, H_A, HEAD_DIM), pr["va32"].reshape(nb, t, H_A, HEAD_DIM),
            pr["logf"].reshape(nb, t, H_A), pr["kb32"].reshape(nb, t, KV_B, HEAD_DIM),
            pr["vb32"].reshape(nb, t, KV_B, HEAD_DIM), pr["ki32"].reshape(nb, t, D_IDX), new_g)
    return y.reshape(nb, t, d), rows


def _block_diag_rows(q, n_blocks):
    nb, heads, tok, hd = q.shape
    per = heads // n_blocks
    blk = jnp.arange(heads) // per
    onehot = (blk[:, None] == jnp.arange(n_blocks)[None, :])
    out = jnp.where(onehot[None, :, None, :, None], q[:, :, :, None, :], jnp.zeros((), q.dtype))
    return out.reshape(nb, heads * tok, n_blocks * hd)


def _sample_layer(x, wts, tables, pools, state, page_table):
    nb, nt, d = x.shape
    assert nt == SUBLANES
    n = nb * nt
    x2d = x.reshape(n, d)
    pr = _project(x2d, wts, tables, tm=n, rope_period=n)
    fox_k, fox_v, fox_lf_t, dsa_k, dsa_v, idx_k = pools
    past_len = page_table.shape[1] * PAGE_SIZE

    def heads_first(a, heads, hd):
        return jnp.swapaxes(a.reshape(nb, nt, heads, hd), 1, 2)

    def new_tile(a):
        return _pad_to(a.reshape(nb, nt, -1), PAGE_SIZE, 1)

    qbd = _block_diag_rows(heads_first(pr["qa16"], H_A, HEAD_DIM), H_A)
    lfnew = _pad_to(jnp.swapaxes(pr["logf"].reshape(nb, nt, H_A), 1, 2), PAGE_SIZE, 2)
    oa = _fox_decode_call(page_table, qbd, pr["logf"], new_tile(pr["ka16"]), new_tile(pr["va16"]), lfnew,
                          fox_k, fox_v, fox_lf_t, nt)
    n_sel = min(TOPK_MAX, (past_len + nt) // 4)
    qi_rows = heads_first(pr["qi16"], H_IDX, D_IDX).reshape(nb, H_IDX * nt, D_IDX)
    wi_rows = jnp.swapaxes(pr["wi"].reshape(nb, nt, H_IDX), 1, 2).reshape(nb, H_IDX * nt, 1)
    sc, thr, jcut = _idx_decode_call(page_table, qi_rows, wi_rows, new_tile(pr["ki16"]), idx_k, nt, n_sel)
    qbd_b = _block_diag_rows(heads_first(pr["qb16"], H_B, HEAD_DIM), KV_B)
    ob = _dsa_decode_call(page_table, qbd_b, new_tile(pr["kb16"]), new_tile(pr["vb16"]), sc, thr, jcut,
                          dsa_k, dsa_v, nt)
    h = _outproj_call(oa, ob, wts["w_out"], x2d, tm=n, tn=512)
    dffp = wts["w_gate"].shape[1]
    st = _pad_to(state, dffp, 2)
    zeros = jnp.zeros((nb, nt, dffp), F32)
    s1 = zeros.at[:, 0].set(st[:, 1]).reshape(n, dffp)
    s2 = zeros.at[:, 0].set(st[:, 0]).at[:, 1].set(st[:, 1]).reshape(n, dffp)
    y, gfull = _ffn_call(h, wts["g_ffn"], wts["w_gate"], wts["w_up"], wts["conv_w"], wts["conv_b"], wts["w_down"],
                         tm=n, tf=512, seq_rows=nt, state=(s1, s2))
    dff = wts["dff"]
    new_g = gfull.reshape(nb, nt, dffp)[:, nt - (CONV_W - 1):, :dff]
    rows = (pr["ka32"].reshape(nb, nt, H_A, HEAD_DIM), pr["va32"].reshape(nb, nt, H_A, HEAD_DIM),
            pr["logf"].reshape(nb, nt, H_A), pr["kb32"].reshape(nb, nt, KV_B, HEAD_DIM),
            pr["vb32"].reshape(nb, nt, KV_B, HEAD_DIM), pr["ki32"].reshape(nb, nt, D_IDX), new_g)
    return y.reshape(nb, nt, d), rows


def kernel(x_prompt, x_sample, cache_fox_k, cache_fox_v, cache_fox_logf, cache_dsa_k, cache_dsa_v, cache_idx_k,
           state_ffn_conv, page_table, w_in, b_f, g_qa, g_ka, g_qb, g_kb, g_attn, w_out, g_ffn, w_gate, w_up,
           conv_w, conv_b, w_down):
    depth = w_in.shape[0]
    t = x_prompt.shape[1]
    nb_s, nt_s = x_sample.shape[:2]
    past_len = page_table.shape[1] * PAGE_SIZE
    pos_p = jnp.arange(t, dtype=I32)
    pos_s = jnp.tile(past_len + jnp.arange(nt_s, dtype=I32), nb_s)
    tables_p = (_rope_tables(pos_p, HEAD_DIM), _rope_tables(pos_p, D_IDX))
    tables_s = (_rope_tables(pos_s, HEAD_DIM), _rope_tables(pos_s, D_IDX))
    n_pool = cache_fox_k.shape[1]
    flat = lambda pool: pool.reshape(depth * n_pool, PAGE_SIZE, -1)
    pools = (flat(cache_fox_k), flat(cache_fox_v), jnp.swapaxes(flat(cache_fox_logf), 1, 2),
             flat(cache_dsa_k), flat(cache_dsa_v), flat(cache_idx_k))
    y_p, y_s = x_prompt, x_sample
    rows_p, rows_s = [], []
    for l in range(depth):
        wts = _layer_weights(l, w_in, b_f, g_qa, g_ka, g_qb, g_kb, g_attn, w_out, g_ffn, w_gate, w_up,
                             conv_w, conv_b, w_down)
        y_p, new_p = _prompt_layer(y_p, wts, tables_p)
        y_s, new_s = _sample_layer(y_s, wts, tables_s, pools, state_ffn_conv[l], page_table + l * n_pool)
        rows_p.append(new_p)
        rows_s.append(new_s)
    stack = lambda vs: vs[0][None] if len(vs) == 1 else jnp.stack(vs)
    out_p = [stack(v) for v in zip(*rows_p)]
    out_s = [stack(v) for v in zip(*rows_s)]
    return (y_p, y_s, *out_p, *out_s)
```

```python
import functools

import jax
import jax.numpy as jnp
import numpy as np
from jax import lax
from jax.experimental import pallas as pl
from jax.experimental.pallas import tpu as pltpu

F32 = jnp.float32
BF16 = jnp.bfloat16
I32 = jnp.int32

LANES = 128
SUBLANES = 8
VMEM_CAP_BYTES = 56 * 1024 * 1024

HEAD_DIM = 128
H_A = 8
H_B = 8
KV_B = 2
GROUP_B = H_B // KV_B
H_IDX = 16
D_IDX = 64
TOPK_MAX = 256
PAGE_SIZE = 128
ROPE_THETA = 10000.0
EPS = 1e-6
CONV_W = 3

NEG = -1e30
INT_MIN = -(2 ** 31)
INT_MAX = 2 ** 31 - 1
NEG_INF_KEY = int(np.int32(np.uint32(0xFF800000) ^ np.uint32(0x7FFFFFFF)))

NT_DIMS = (((1,), (1,)), ((), ()))


def _nt_dot(a, b):
    return lax.dot_general(a, b, NT_DIMS, preferred_element_type=F32)


def _dot(a, b):
    return jnp.dot(a, b, preferred_element_type=F32)


def _params(semantics, vmem_bytes):
    limit = int(min(max(vmem_bytes, 16 * 1024 * 1024), VMEM_CAP_BYTES))
    return pltpu.CompilerParams(dimension_semantics=semantics, vmem_limit_bytes=limit)


def _nbytes(shape, dtype):
    return int(np.prod(shape)) * jnp.dtype(dtype).itemsize


def _f32_to_key(x):
    b = lax.bitcast_convert_type(x, I32)
    return b ^ ((b >> 31) & INT_MAX)


def _rms_kernel(x_ref, g_ref, o_ref):
    x = x_ref[...]
    y = x * lax.rsqrt(jnp.mean(x * x, axis=-1, keepdims=True) + EPS)
    o_ref[...] = (y * g_ref[...]).astype(o_ref.dtype)


def _rms_call(x, g, tm):
    n, d = x.shape
    return pl.pallas_call(
        _rms_kernel,
        out_shape=jax.ShapeDtypeStruct((n, d), BF16),
        grid=(n // tm,),
        in_specs=[pl.BlockSpec((tm, d), lambda i: (i, 0)),
                  pl.BlockSpec((1, d), lambda i: (0, 0))],
        out_specs=pl.BlockSpec((tm, d), lambda i: (i, 0)),
        compiler_params=_params(("parallel",), 2 * (_nbytes((tm, d), F32) + _nbytes((tm, d), BF16)) + (4 << 20)),
        name="rms_norm",
    )(x, g.reshape(1, d))


def _rope_lanes(y, cos, sin, dim):
    if dim == LANES:
        rot = pltpu.roll(y, LANES // 2, 1)
    else:
        lane = lax.broadcasted_iota(I32, y.shape, 1)
        first_half = (lane % dim) < (dim // 2)
        rot = jnp.where(first_half, pltpu.roll(y, LANES - dim // 2, 1), pltpu.roll(y, dim // 2, 1))
    return y * cos + rot * sin


def _proj_kernel(*refs, has_gain, rope_dim, n32, n16):
    refs = list(refs)
    xn_ref, w_ref = refs[0], refs[1]
    pos = 2
    gain_ref = cos_ref = sin_ref = None
    if has_gain:
        gain_ref = refs[pos]
        pos += 1
    if rope_dim:
        cos_ref, sin_ref = refs[pos], refs[pos + 1]
        pos += 2
    out_refs = refs[pos:pos + n32 + n16]
    acc = _dot(xn_ref[...], w_ref[...])
    tn = acc.shape[1]
    for c in range(tn // LANES):
        cs = slice(c * LANES, (c + 1) * LANES)
        y = acc[:, cs]
        if has_gain:
            y = y * lax.rsqrt(jnp.mean(y * y, axis=-1, keepdims=True) + EPS) * gain_ref[:, cs]
        if rope_dim:
            y = _rope_lanes(y, cos_ref[...], sin_ref[...], rope_dim)
        for o in out_refs:
            o[:, cs] = y.astype(o.dtype)


def _proj_call(xn, w16, col0, ncols, tm, tn, *, gain=None, rope=None, rope_period=None, want32, want16, name):
    n, d = xn.shape
    assert col0 % tn == 0 and ncols % tn == 0 and n % tm == 0
    cb0 = col0 // tn
    in_specs = [pl.BlockSpec((tm, d), lambda i, j: (i, 0)),
                pl.BlockSpec((d, tn), lambda i, j: (0, cb0 + j))]
    args = [xn, w16]
    if gain is not None:
        in_specs.append(pl.BlockSpec((1, tn), lambda i, j: (0, j)))
        args.append(gain.reshape(1, ncols))
    rope_dim = 0
    if rope is not None:
        cos, sin, rope_dim = rope
        nper = rope_period // tm
        spec = pl.BlockSpec((tm, LANES), lambda i, j: (i % nper, 0))
        in_specs += [spec, spec]
        args += [cos, sin]
    out_shape, out_specs = [], []
    for dt, want in ((F32, want32), (BF16, want16)):
        if want:
            out_shape.append(jax.ShapeDtypeStruct((n, ncols), dt))
            out_specs.append(pl.BlockSpec((tm, tn), lambda i, j: (i, j)))
    vmem = 2 * (_nbytes((tm, d), BF16) + _nbytes((d, tn), BF16) + 2 * _nbytes((tm, tn), F32)) + 4 * _nbytes((tm, tn), F32)
    outs = pl.pallas_call(
        functools.partial(_proj_kernel, has_gain=gain is not None, rope_dim=rope_dim,
                          n32=int(want32), n16=int(want16)),
        out_shape=out_shape,
        grid=(n // tm, ncols // tn),
        in_specs=in_specs,
        out_specs=out_specs,
        compiler_params=_params(("parallel", "parallel"), vmem + (8 << 20)),
        name=name,
    )(*args)
    return outs


def _misc_kernel(xn_ref, w_ref, bf_ref, cos_ref, sin_ref, ki32_ref, ki16_ref, logf_ref, wi_ref):
    acc = _dot(xn_ref[...], w_ref[...])
    ki = _rope_lanes(acc[:, :LANES], cos_ref[...], sin_ref[...], D_IDX)[:, :D_IDX]
    ki32_ref[...] = ki
    ki16_ref[...] = ki.astype(BF16)
    z = acc[:, LANES:LANES + H_A] + bf_ref[...]
    logf_ref[...] = jnp.minimum(z, 0.0) - jnp.log1p(jnp.exp(-jnp.abs(z)))
    wi_ref[...] = acc[:, 2 * LANES:2 * LANES + H_IDX]


def _misc_call(xn, w_misc, b_f, cos, sin, tm, rope_period):
    n, d = xn.shape
    nper = rope_period // tm
    tspec = pl.BlockSpec((tm, LANES), lambda i: (i % nper, 0))
    return pl.pallas_call(
        _misc_kernel,
        out_shape=[jax.ShapeDtypeStruct((n, D_IDX), F32), jax.ShapeDtypeStruct((n, D_IDX), BF16),
                   jax.ShapeDtypeStruct((n, H_A), F32), jax.ShapeDtypeStruct((n, H_IDX), F32)],
        grid=(n // tm,),
        in_specs=[pl.BlockSpec((tm, d), lambda i: (i, 0)),
                  pl.BlockSpec((d, 3 * LANES), lambda i: (0, 0)),
                  pl.BlockSpec((1, H_A), lambda i: (0, 0)),
                  tspec, tspec],
        out_specs=[pl.BlockSpec((tm, D_IDX), lambda i: (i, 0)), pl.BlockSpec((tm, D_IDX), lambda i: (i, 0)),
                   pl.BlockSpec((tm, H_A), lambda i: (i, 0)), pl.BlockSpec((tm, H_IDX), lambda i: (i, 0))],
        compiler_params=_params(("parallel",), 32 << 20),
        name="proj_misc",
    )(xn, w_misc, b_f.reshape(1, H_A), cos, sin)


def _lane_prefix(x):
    lane = lax.broadcasted_iota(I32, x.shape, 1)
    k = 1
    while k < LANES:
        x = x + jnp.where(lane >= k, pltpu.roll(x, k, 1), 0.0)
        k *= 2
    return x


def _lane_suffix(x):
    lane = lax.broadcasted_iota(I32, x.shape, 1)
    k = 1
    while k < LANES:
        x = x + jnp.where(lane < LANES - k, pltpu.roll(x, LANES - k, 1), 0.0)
        k *= 2
    return x


def _cumsum_kernel(x_ref, o_ref):
    t = x_ref.shape[2]
    carry = jnp.zeros((x_ref.shape[1], 1), F32)
    for c in range(t // LANES):
        cs = slice(c * LANES, (c + 1) * LANES)
        y = _lane_prefix(x_ref[0, :, cs]) + carry
        o_ref[0, :, cs] = y
        carry = y[:, LANES - 1:LANES]


def _cumsum_call(logf_t):
    b, h, t = logf_t.shape
    return pl.pallas_call(
        _cumsum_kernel,
        out_shape=jax.ShapeDtypeStruct((b, h, t), F32),
        grid=(b,),
        in_specs=[pl.BlockSpec((1, h, t), lambda i: (i, 0, 0))],
        out_specs=pl.BlockSpec((1, h, t), lambda i: (i, 0, 0)),
        compiler_params=_params(("parallel",), 16 << 20),
        name="logf_cumsum",
    )(logf_t)


def _softmax_step(s, mask, h, hs, v, m_ref, l_ref, acc_ref):
    if mask is not None:
        s = jnp.where(mask, s, NEG)
    m_prev = m_ref[h]
    m_new = jnp.maximum(m_prev, jnp.max(s, axis=-1, keepdims=True))
    alpha = jnp.exp(m_prev - m_new)
    p = jnp.exp(s - m_new)
    if mask is not None:
        p = jnp.where(mask, p, 0.0)
    l_ref[h] = alpha * l_ref[h] + jnp.sum(p, axis=-1, keepdims=True)
    acc_ref[:, hs] = alpha * acc_ref[:, hs] + _dot(p.astype(BF16), v)
    m_ref[h] = m_new


def _init_softmax(m_ref, l_ref, acc_ref):
    m_ref[...] = jnp.full(m_ref.shape, NEG, F32)
    l_ref[...] = jnp.zeros(l_ref.shape, F32)
    acc_ref[...] = jnp.zeros(acc_ref.shape, F32)


def _fox_kernel(q_ref, k_ref, v_ref, cq_ref, ck_ref, o_ref, m_ref, l_ref, acc_ref, *, scale):
    i, j = pl.program_id(1), pl.program_id(2)
    tq, tk = q_ref.shape[0], k_ref.shape[0]

    @pl.when(j == 0)
    def _():
        _init_softmax(m_ref, l_ref, acc_ref)

    def step(diag):
        mask = None
        if diag:
            mask = lax.broadcasted_iota(I32, (tq, tk), 0) >= lax.broadcasted_iota(I32, (tq, tk), 1)
        qk = lambda h: _nt_dot(q_ref[:, h * HEAD_DIM:(h + 1) * HEAD_DIM], k_ref[:, h * HEAD_DIM:(h + 1) * HEAD_DIM])
        s_next = qk(0)
        for h in range(H_A):
            hs = slice(h * HEAD_DIM, (h + 1) * HEAD_DIM)
            s = s_next * scale
            if h + 1 < H_A:
                s_next = qk(h + 1)
            s = s + cq_ref[:, h:h + 1] - ck_ref[0, h:h + 1, :]
            _softmax_step(s, mask, h, hs, v_ref[:, hs], m_ref, l_ref, acc_ref)

    pl.when(j < i)(functools.partial(step, False))

    @pl.when(j == i)
    def _():
        step(True)
        for h in range(H_A):
            hs = slice(h * HEAD_DIM, (h + 1) * HEAD_DIM)
            o_ref[:, hs] = (acc_ref[:, hs] / l_ref[h]).astype(o_ref.dtype)


def _fox_call(q16, k16, v16, c_rows, c_lanes, nb, t, tq):
    nq = t // tq
    width = H_A * HEAD_DIM
    qmap = lambda b, i, j: (b * nq + i, 0)
    kmap = lambda b, i, j: (b * nq + jnp.minimum(j, i), 0)
    vmem = 2 * 4 * _nbytes((tq, width), BF16) + _nbytes((tq, width), F32) + 2 * H_A * _nbytes((tq, LANES), F32)
    return pl.pallas_call(
        functools.partial(_fox_kernel, scale=HEAD_DIM ** -0.5),
        out_shape=jax.ShapeDtypeStruct((nb * t, width), BF16),
        grid=(nb, nq, nq),
        in_specs=[pl.BlockSpec((tq, width), qmap),
                  pl.BlockSpec((tq, width), kmap),
                  pl.BlockSpec((tq, width), kmap),
                  pl.BlockSpec((tq, H_A), qmap),
                  pl.BlockSpec((1, H_A, tq), lambda b, i, j: (b, 0, jnp.minimum(j, i)))],
        out_specs=pl.BlockSpec((tq, width), qmap),
        scratch_shapes=[pltpu.VMEM((H_A, tq, 1), F32), pltpu.VMEM((H_A, tq, 1), F32),
                        pltpu.VMEM((tq, width), F32)],
        compiler_params=_params(("parallel", "parallel", "arbitrary"), vmem + (16 << 20)),
        name="fox_prompt_attention",
    )(q16, k16, v16, c_rows, c_lanes)


def _count_where(key_ref, n_chunks, chunk, rows, pred):
    def body(c, acc):
        start = pl.multiple_of(c * chunk, chunk)
        blk = key_ref[:, pl.ds(start, chunk)]
        for s in range(chunk // LANES):
            col = start + s * LANES + lax.broadcasted_iota(I32, (rows, LANES), 1)
            acc = jnp.where(pred(blk[:, s * LANES:(s + 1) * LANES], col), acc + 1.0, acc)
        return acc
    acc = lax.fori_loop(0, n_chunks, body, jnp.zeros((rows, LANES), F32))
    return jnp.sum(acc, axis=-1, keepdims=True)


def _select_threshold(key_ref, n_chunks, chunk, rows, n_sel, thr_ref, jcut_ref):
    nf = float(n_sel)

    def count_ge(cand):
        return _count_where(key_ref, n_chunks, chunk, rows, lambda k, col: k >= cand)

    base = jnp.where(count_ge(jnp.zeros((rows, 1), I32)) >= nf, 0, INT_MIN).astype(I32)

    def bit_body(b, base):
        cand = base | jnp.left_shift(jnp.int32(1), 30 - b)
        return jnp.where(count_ge(cand) >= nf, cand, base)

    base = lax.fori_loop(0, 31, bit_body, base)
    n_ge = count_ge(base)
    n_gt = count_ge(base + 1)
    need = jnp.logical_and(n_ge > nf, base != NEG_INF_KEY)
    thr_ref[...] = base.reshape(thr_ref.shape)
    jcut_ref[...] = jnp.full(jcut_ref.shape, INT_MAX, I32)

    @pl.when(jnp.max(jnp.where(need, 1.0, 0.0)) > 0.0)
    def _():
        want = nf - n_gt

        def count_eq_below(x):
            return _count_where(key_ref, n_chunks, chunk, rows,
                                lambda k, col: jnp.logical_and(k == base, col < x))

        def col_body(b, x):
            cand = x | jnp.left_shift(jnp.int32(1), 14 - b)
            return jnp.where(count_eq_below(cand) < want, cand, x)

        x = lax.fori_loop(0, 15, col_body, jnp.zeros((rows, 1), I32))
        jcut_ref[...] = jnp.where(need, x, INT_MAX).reshape(jcut_ref.shape)


def _canon_zero(x):
    return jnp.where(x == 0.0, 0.0, x)


def _idx_kernel(qi_ref, wi_ref, ki_ref, sc_ref, thr_ref, jcut_ref, key_ref, *, n_sel, idx_scale):
    i, j = pl.program_id(1), pl.program_id(2)
    tq, tk = qi_ref.shape[0], ki_ref.shape[0]

    def scores(diag):
        acc = jnp.zeros((tq, tk), F32)
        for h in range(H_IDX):
            d = _nt_dot(qi_ref[:, h * D_IDX:(h + 1) * D_IDX], ki_ref[...])
            acc = acc + wi_ref[:, h:h + 1] * jnp.maximum(d, 0.0)
        sc = _canon_zero(acc * idx_scale)
        if diag:
            causal = lax.broadcasted_iota(I32, (tq, tk), 0) >= lax.broadcasted_iota(I32, (tq, tk), 1)
            sc = jnp.where(causal, sc, -jnp.inf)
        key = _f32_to_key(sc)
        sc_ref[...] = key
        key_ref[:, pl.ds(pl.multiple_of(j * tk, tk), tk)] = key

    pl.when(j < i)(functools.partial(scores, False))

    @pl.when(j == i)
    def _():
        scores(True)
        _select_threshold(key_ref, i + 1, tk, tq, n_sel, thr_ref, jcut_ref)


def _idx_call(qi16, wi, ki16, nb, t, tq, n_sel):
    nq = t // tq
    qmap = lambda b, i, j: (b * nq + i, 0)
    vmem = _nbytes((tq, t), I32) + 2 * (_nbytes((tq, H_IDX * D_IDX), BF16) + 2 * _nbytes((tq, tq), F32))
    return pl.pallas_call(
        functools.partial(_idx_kernel, n_sel=n_sel, idx_scale=(H_IDX * D_IDX) ** -0.5),
        out_shape=[jax.ShapeDtypeStruct((nb * t, t), I32),
                   jax.ShapeDtypeStruct((nb * t, 1), I32),
                   jax.ShapeDtypeStruct((nb * t, 1), I32)],
        grid=(nb, nq, nq),
        in_specs=[pl.BlockSpec((tq, H_IDX * D_IDX), qmap),
                  pl.BlockSpec((tq, H_IDX), qmap),
                  pl.BlockSpec((tq, D_IDX), lambda b, i, j: (b * nq + jnp.minimum(j, i), 0))],
        out_specs=[pl.BlockSpec((tq, tq), lambda b, i, j: (b * nq + i, jnp.minimum(j, i))),
                   pl.BlockSpec((tq, 1), qmap),
                   pl.BlockSpec((tq, 1), qmap)],
        scratch_shapes=[pltpu.VMEM((tq, t), I32)],
        compiler_params=_params(("parallel", "parallel", "arbitrary"), vmem + (16 << 20)),
        name="dsa_prompt_indexer",
    )(qi16, wi, ki16)


def _selected(key, thr, jcut, col):
    return jnp.logical_or(key > thr, jnp.logical_and(key == thr, col <= jcut))


def _dsa_kernel(q_ref, k_ref, v_ref, sc_ref, thr_ref, jcut_ref, o_ref, m_ref, l_ref, acc_ref, *, scale):
    i, j = pl.program_id(1), pl.program_id(2)
    tq, tk = q_ref.shape[0], k_ref.shape[0]

    @pl.when(j == 0)
    def _():
        _init_softmax(m_ref, l_ref, acc_ref)

    @pl.when(j <= i)
    def _():
        row = i * tq + lax.broadcasted_iota(I32, (tq, tk), 0)
        col = j * tk + lax.broadcasted_iota(I32, (tq, tk), 1)
        mask = jnp.logical_and(row >= col, _selected(sc_ref[...], thr_ref[...], jcut_ref[...], col))
        kv = lambda h: slice((h // GROUP_B) * HEAD_DIM, (h // GROUP_B + 1) * HEAD_DIM)
        qk = lambda h: _nt_dot(q_ref[:, h * HEAD_DIM:(h + 1) * HEAD_DIM], k_ref[:, kv(h)])
        s_next = qk(0)
        for h in range(H_B):
            hs = slice(h * HEAD_DIM, (h + 1) * HEAD_DIM)
            s = s_next * scale
            if h + 1 < H_B:
                s_next = qk(h + 1)
            _softmax_step(s, mask, h, hs, v_ref[:, kv(h)], m_ref, l_ref, acc_ref)

    @pl.when(j == i)
    def _():
        for h in range(H_B):
            hs = slice(h * HEAD_DIM, (h + 1) * HEAD_DIM)
            o_ref[:, hs] = (acc_ref[:, hs] / l_ref[h]).astype(o_ref.dtype)


def _dsa_call(q16, k16, v16, sc, thr, jcut, nb, t, tq):
    nq = t // tq
    qw, kw = H_B * HEAD_DIM, KV_B * HEAD_DIM
    qmap = lambda b, i, j: (b * nq + i, 0)
    kmap = lambda b, i, j: (b * nq + jnp.minimum(j, i), 0)
    vmem = 2 * (2 * _nbytes((tq, qw), BF16) + 2 * _nbytes((tq, kw), BF16) + _nbytes((tq, tq), F32)) \
        + _nbytes((tq, qw), F32) + 2 * H_B * _nbytes((tq, LANES), F32)
    return pl.pallas_call(
        functools.partial(_dsa_kernel, scale=HEAD_DIM ** -0.5),
        out_shape=jax.ShapeDtypeStruct((nb * t, qw), BF16),
        grid=(nb, nq, nq),
        in_specs=[pl.BlockSpec((tq, qw), qmap),
                  pl.BlockSpec((tq, kw), kmap),
                  pl.BlockSpec((tq, kw), kmap),
                  pl.BlockSpec((tq, tq), lambda b, i, j: (b * nq + i, jnp.minimum(j, i))),
                  pl.BlockSpec((tq, 1), qmap),
                  pl.BlockSpec((tq, 1), qmap)],
        out_specs=pl.BlockSpec((tq, qw), qmap),
        scratch_shapes=[pltpu.VMEM((H_B, tq, 1), F32), pltpu.VMEM((H_B, tq, 1), F32),
                        pltpu.VMEM((tq, qw), F32)],
        compiler_params=_params(("parallel", "parallel", "arbitrary"), vmem + (16 << 20)),
        name="dsa_prompt_attention",
    )(q16, k16, v16, sc, thr, jcut)


FOX_PAGES_PER_STEP = 8
DSA_PAGES_PER_STEP = 16


def _rows_from_heads(x, reps):
    return jnp.concatenate([jnp.broadcast_to(x[h:h + 1, :], (reps, x.shape[1])) for h in range(x.shape[0])], axis=0)


def _decode_softmax_step(s, mask, pv_fn, m_ref, l_ref, acc_ref):
    if mask is not None:
        s = jnp.where(mask, s, NEG)
    m_prev = m_ref[...]
    m_new = jnp.maximum(m_prev, jnp.max(s, axis=-1, keepdims=True))
    alpha = jnp.exp(m_prev - m_new)
    p = jnp.exp(s - m_new)
    if mask is not None:
        p = jnp.where(mask, p, 0.0)
    l_ref[...] = alpha * l_ref[...] + jnp.sum(p, axis=-1, keepdims=True)
    acc_ref[...] = alpha * acc_ref[...] + pv_fn(p)
    m_ref[...] = m_new


def _group_logits(q_ref, groups, rows_per_group, k_tile_fn, n_tiles):
    out = []
    for g in range(groups):
        qg = q_ref[0, g * rows_per_group:(g + 1) * rows_per_group, :].astype(BF16)
        out.append(jnp.concatenate([_nt_dot(qg, k_tile_fn(t, g)) for t in range(n_tiles)], axis=1))
    return jnp.concatenate(out, axis=0)


def _group_pv(p, groups, rows_per_group, v_tile_fn, n_tiles):
    out = []
    for g in range(groups):
        pg = p[g * rows_per_group:(g + 1) * rows_per_group, :]
        acc = _dot(pg[:, :PAGE_SIZE].astype(BF16), v_tile_fn(0, g))
        for t in range(1, n_tiles):
            acc = acc + _dot(pg[:, t * PAGE_SIZE:(t + 1) * PAGE_SIZE].astype(BF16), v_tile_fn(t, g))
        out.append(acc)
    return jnp.concatenate(out, axis=0)


def _pool_tile(ref, g, groups):
    return ref[0, pl.ds(g, PAGE_SIZE, stride=groups), :].astype(BF16)


def _fox_decode_kernel(pt_ref, q_ref, lfq_ref, knew_ref, vnew_ref, lfnew_ref, *rest, n_tok, scale):
    del pt_ref
    npg = FOX_PAGES_PER_STEP
    k_refs, v_refs, lf_refs = rest[:npg], rest[npg:2 * npg], rest[2 * npg:3 * npg]
    o_ref, m_ref, l_ref, acc_ref, carry_ref, rowc_ref = rest[3 * npg:]
    s_id = pl.program_id(1)
    rows = H_A * n_tok
    lane_block = lambda ref, h: ref[0, :, h * HEAD_DIM:(h + 1) * HEAD_DIM]

    @pl.when(s_id == 0)
    def _():
        _init_softmax(m_ref, l_ref, acc_ref)
        lf = lfq_ref[...]
        tok = lax.broadcasted_iota(I32, lf.shape, 0)
        after = jnp.zeros(lf.shape, F32)
        for jj in range(1, n_tok):
            after = after + jnp.where(tok < jj, lf[jj:jj + 1, :], 0.0)
        rowc_ref[...] = jnp.concatenate([-after[:, h:h + 1] for h in range(H_A)], axis=0)
        lfn = lfnew_ref[0]
        incl = _lane_suffix(lfn)
        tokq = lax.broadcasted_iota(I32, (rows, LANES), 0) % n_tok
        mask = lax.broadcasted_iota(I32, (rows, LANES), 1) <= tokq
        s = _group_logits(q_ref, H_A, n_tok, lambda t, h: lane_block(knew_ref, h), 1) * scale
        s = s + rowc_ref[...] + _rows_from_heads(incl - lfn, n_tok)
        _decode_softmax_step(s, mask, lambda p: _group_pv(p, H_A, n_tok, lambda t, h: lane_block(vnew_ref, h), 1),
                             m_ref, l_ref, acc_ref)
        carry_ref[...] = incl[:, 0:1]

    @pl.when(s_id > 0)
    def _():
        carry = carry_ref[...]
        bias = [None] * npg
        for p in reversed(range(npg)):
            lf = lf_refs[p][0]
            incl = _lane_suffix(lf)
            bias[p] = _rows_from_heads(incl - lf + carry, n_tok)
            carry = carry + incl[:, 0:1]
        carry_ref[...] = carry
        s = _group_logits(q_ref, H_A, n_tok, lambda t, h: _pool_tile(k_refs[t], h, H_A), npg) * scale
        s = s + rowc_ref[...] + jnp.concatenate(bias, axis=1)
        _decode_softmax_step(s, None, lambda p: _group_pv(p, H_A, n_tok, lambda t, h: _pool_tile(v_refs[t], h, H_A), npg),
                             m_ref, l_ref, acc_ref)

    @pl.when(s_id == pl.num_programs(1) - 1)
    def _():
        for h in range(H_A):
            rs = slice(h * n_tok, (h + 1) * n_tok)
            o_ref[:, h * HEAD_DIM:(h + 1) * HEAD_DIM] = (acc_ref[rs, :] / l_ref[rs, :]).astype(o_ref.dtype)


def _page_specs(block, npg, group_of_step):
    specs = []
    for p in range(npg):
        def imap(b, s, pt, p=p):
            return (pt[b, group_of_step(s) * npg + p],) + (0,) * (len(block) - 1)
        specs.append(pl.BlockSpec(block, imap))
    return specs


def _fox_decode_call(page_table, q_rows, lf_rows, knew, vnew, lfnew, k_pool, v_pool, lf_pool, n_tok):
    nb, n_pages = page_table.shape
    npg = FOX_PAGES_PER_STEP
    n_groups = n_pages // npg
    assert n_groups * npg == n_pages
    rows, width = H_A * n_tok, H_A * HEAD_DIM
    bmap = lambda b, s, pt: (b, 0, 0)
    mirrored = lambda s: n_groups - 1 - jnp.maximum(s - 1, 0)
    page_rows = PAGE_SIZE * H_A
    in_specs = [pl.BlockSpec((1, rows, HEAD_DIM), bmap),
                pl.BlockSpec((n_tok, H_A), lambda b, s, pt: (b, 0)),
                pl.BlockSpec((1, PAGE_SIZE, width), bmap),
                pl.BlockSpec((1, PAGE_SIZE, width), bmap),
                pl.BlockSpec((1, H_A, PAGE_SIZE), bmap)]
    in_specs += _page_specs((1, page_rows, HEAD_DIM), npg, mirrored)
    in_specs += _page_specs((1, page_rows, HEAD_DIM), npg, mirrored)
    in_specs += _page_specs((1, H_A, PAGE_SIZE), npg, mirrored)
    vmem = 2 * 2 * npg * _nbytes((page_rows, HEAD_DIM), F32) + 3 * npg * _nbytes((page_rows, HEAD_DIM), BF16) \
        + 8 * _nbytes((rows, npg * PAGE_SIZE), F32)
    return pl.pallas_call(
        functools.partial(_fox_decode_kernel, n_tok=n_tok, scale=HEAD_DIM ** -0.5),
        out_shape=jax.ShapeDtypeStruct((nb * n_tok, width), BF16),
        grid_spec=pltpu.PrefetchScalarGridSpec(
            num_scalar_prefetch=1,
            grid=(nb, n_groups + 1),
            in_specs=in_specs,
            out_specs=pl.BlockSpec((n_tok, width), lambda b, s, pt: (b, 0)),
            scratch_shapes=[pltpu.VMEM((rows, 1), F32), pltpu.VMEM((rows, 1), F32),
                            pltpu.VMEM((rows, HEAD_DIM), F32), pltpu.VMEM((H_A, 1), F32),
                            pltpu.VMEM((rows, 1), F32)]),
        compiler_params=_params(("parallel", "arbitrary"), vmem + (8 << 20)),
        name="fox_decode_attention",
    )(page_table, q_rows, lf_rows, knew, vnew, lfnew, *([k_pool] * npg), *([v_pool] * npg), *([lf_pool] * npg))


def _idx_decode_kernel(pt_ref, qi_ref, wi_ref, kinew_ref, *rest, n_tok, n_sel, idx_scale):
    del pt_ref
    npg = DSA_PAGES_PER_STEP
    ki_refs = rest[:npg]
    sc_ref, thr_ref, jcut_ref, key_ref = rest[npg:]
    s_id = pl.program_id(1)
    last = pl.num_programs(1) - 1
    width = npg * PAGE_SIZE

    def page_scores(ki16):
        d = jnp.maximum(_nt_dot(qi_ref[0], ki16), 0.0) * wi_ref[0]
        acc = jnp.zeros((n_tok, PAGE_SIZE), F32)
        for h in range(H_IDX):
            acc = acc + d[h * n_tok:(h + 1) * n_tok, :]
        return _canon_zero(acc * idx_scale)

    def put(p, key):
        sc_ref[0, :, p * PAGE_SIZE:(p + 1) * PAGE_SIZE] = key
        key_ref[:, pl.ds(pl.multiple_of(s_id * width + p * PAGE_SIZE, PAGE_SIZE), PAGE_SIZE)] = key

    @pl.when(s_id < last)
    def _():
        for p in range(npg):
            put(p, _f32_to_key(page_scores(ki_refs[p][0].astype(BF16))))

    @pl.when(s_id == last)
    def _():
        sc = page_scores(kinew_ref[0])
        causal = lax.broadcasted_iota(I32, sc.shape, 1) <= lax.broadcasted_iota(I32, sc.shape, 0)
        put(0, _f32_to_key(jnp.where(causal, sc, -jnp.inf)))
        for p in range(1, npg):
            put(p, jnp.full((n_tok, PAGE_SIZE), NEG_INF_KEY, I32))
        _select_threshold(key_ref, s_id + 1, width, n_tok, n_sel, thr_ref, jcut_ref)


def _idx_decode_call(page_table, qi_rows, wi_rows, kinew, ki_pool, n_tok, n_sel):
    nb, n_pages = page_table.shape
    npg = DSA_PAGES_PER_STEP
    n_groups = n_pages // npg
    assert n_groups * npg == n_pages
    width = npg * PAGE_SIZE
    n_cols = (n_groups + 1) * width
    bmap = lambda b, s, pt: (b, 0, 0)
    forward = lambda s: jnp.minimum(s, n_groups - 1)
    in_specs = [pl.BlockSpec((1, H_IDX * n_tok, D_IDX), bmap),
                pl.BlockSpec((1, H_IDX * n_tok, 1), bmap),
                pl.BlockSpec((1, PAGE_SIZE, D_IDX), bmap)]
    in_specs += _page_specs((1, PAGE_SIZE, D_IDX), npg, forward)
    return pl.pallas_call(
        functools.partial(_idx_decode_kernel, n_tok=n_tok, n_sel=n_sel, idx_scale=(H_IDX * D_IDX) ** -0.5),
        out_shape=[jax.ShapeDtypeStruct((nb, n_tok, n_cols), I32),
                   jax.ShapeDtypeStruct((nb, n_tok, 1), I32),
                   jax.ShapeDtypeStruct((nb, n_tok, 1), I32)],
        grid_spec=pltpu.PrefetchScalarGridSpec(
            num_scalar_prefetch=1,
            grid=(nb, n_groups + 1),
            in_specs=in_specs,
            out_specs=[pl.BlockSpec((1, n_tok, width), lambda b, s, pt: (b, 0, s)),
                       pl.BlockSpec((1, n_tok, 1), bmap),
                       pl.BlockSpec((1, n_tok, 1), bmap)],
            scratch_shapes=[pltpu.VMEM((n_tok, n_cols), I32)]),
        compiler_params=_params(("parallel", "arbitrary"), 24 << 20),
        name="dsa_decode_indexer",
    )(page_table, qi_rows, wi_rows, kinew, *([ki_pool] * npg))


def _dsa_decode_kernel(pt_ref, q_ref, knew_ref, vnew_ref, sc_ref, thr_ref, jcut_ref, *rest, n_tok, scale):
    del pt_ref
    npg = DSA_PAGES_PER_STEP
    k_refs, v_refs = rest[:npg], rest[npg:2 * npg]
    o_ref, m_ref, l_ref, acc_ref = rest[2 * npg:]
    s_id = pl.program_id(1)
    last = pl.num_programs(1) - 1
    width = npg * PAGE_SIZE
    grows = GROUP_B * n_tok
    lane_block = lambda ref, g: ref[0, :, g * HEAD_DIM:(g + 1) * HEAD_DIM]

    @pl.when(s_id == 0)
    def _():
        _init_softmax(m_ref, l_ref, acc_ref)

    def tile_mask(p, extra):
        col = s_id * width + p * PAGE_SIZE + lax.broadcasted_iota(I32, (n_tok, PAGE_SIZE), 1)
        sel = _selected(sc_ref[0, :, p * PAGE_SIZE:(p + 1) * PAGE_SIZE], thr_ref[0], jcut_ref[0], col)
        if extra is not None:
            sel = jnp.logical_and(sel, extra)
        return jnp.concatenate([jnp.where(sel, 1.0, 0.0)] * H_B, axis=0)

    @pl.when(s_id < last)
    def _():
        s = _group_logits(q_ref, KV_B, grows, lambda t, g: _pool_tile(k_refs[t], g, KV_B), npg) * scale
        mask = jnp.concatenate([tile_mask(p, None) for p in range(npg)], axis=1) > 0.5
        _decode_softmax_step(s, mask, lambda p: _group_pv(p, KV_B, grows, lambda t, g: _pool_tile(v_refs[t], g, KV_B), npg),
                             m_ref, l_ref, acc_ref)

    @pl.when(s_id == last)
    def _():
        causal = lax.broadcasted_iota(I32, (n_tok, PAGE_SIZE), 1) <= lax.broadcasted_iota(I32, (n_tok, PAGE_SIZE), 0)
        s = _group_logits(q_ref, KV_B, grows, lambda t, g: lane_block(knew_ref, g), 1) * scale
        _decode_softmax_step(s, tile_mask(0, causal) > 0.5,
                             lambda p: _group_pv(p, KV_B, grows, lambda t, g: lane_block(vnew_ref, g), 1),
                             m_ref, l_ref, acc_ref)
        for h in range(H_B):
            rs = slice(h * n_tok, (h + 1) * n_tok)
            o_ref[:, h * HEAD_DIM:(h + 1) * HEAD_DIM] = (acc_ref[rs, :] / l_ref[rs, :]).astype(o_ref.dtype)


def _dsa_decode_call(page_table, q_rows, knew, vnew, sc, thr, jcut, k_pool, v_pool, n_tok):
    nb, n_pages = page_table.shape
    npg = DSA_PAGES_PER_STEP
    n_groups = n_pages // npg
    assert n_groups * npg == n_pages
    width = npg * PAGE_SIZE
    rows, kw = H_B * n_tok, KV_B * HEAD_DIM
    page_rows = PAGE_SIZE * KV_B
    bmap = lambda b, s, pt: (b, 0, 0)
    forward = lambda s: jnp.minimum(s, n_groups - 1)
    in_specs = [pl.BlockSpec((1, rows, HEAD_DIM), bmap),
                pl.BlockSpec((1, PAGE_SIZE, kw), bmap),
                pl.BlockSpec((1, PAGE_SIZE, kw), bmap),
                pl.BlockSpec((1, n_tok, width), lambda b, s, pt: (b, 0, s)),
                pl.BlockSpec((1, n_tok, 1), bmap),
                pl.BlockSpec((1, n_tok, 1), bmap)]
    in_specs += _page_specs((1, page_rows, HEAD_DIM), npg, forward)
    in_specs += _page_specs((1, page_rows, HEAD_DIM), npg, forward)
    return pl.pallas_call(
        functools.partial(_dsa_decode_kernel, n_tok=n_tok, scale=HEAD_DIM ** -0.5),
        out_shape=jax.ShapeDtypeStruct((nb * n_tok, H_B * HEAD_DIM), BF16),
        grid_spec=pltpu.PrefetchScalarGridSpec(
            num_scalar_prefetch=1,
            grid=(nb, n_groups + 1),
            in_specs=in_specs,
            out_specs=pl.BlockSpec((n_tok, H_B * HEAD_DIM), lambda b, s, pt: (b, 0)),
            scratch_shapes=[pltpu.VMEM((rows, 1), F32), pltpu.VMEM((rows, 1), F32),
                            pltpu.VMEM((rows, HEAD_DIM), F32)]),
        compiler_params=_params(("parallel", "arbitrary"), 32 << 20),
        name="dsa_decode_attention",
    )(page_table, q_rows, knew, vnew, sc, thr, jcut, *([k_pool] * npg), *([v_pool] * npg))


def _outproj_kernel(oa_ref, ob_ref, w_ref, x_ref, h_ref):
    ka = oa_ref.shape[1]
    h_ref[...] = x_ref[...] + _dot(oa_ref[...], w_ref[:ka, :]) + _dot(ob_ref[...], w_ref[ka:, :])


def _outproj_call(oa, ob, w16, x, tm, tn):
    n, d = x.shape
    ka, kb = oa.shape[1], ob.shape[1]
    vmem = 2 * (_nbytes((tm, ka + kb), BF16) + _nbytes((ka + kb, tn), BF16) + 2 * _nbytes((tm, tn), F32))
    return pl.pallas_call(
        _outproj_kernel,
        out_shape=jax.ShapeDtypeStruct((n, d), F32),
        grid=(n // tm, d // tn),
        in_specs=[pl.BlockSpec((tm, ka), lambda i, j: (i, 0)),
                  pl.BlockSpec((tm, kb), lambda i, j: (i, 0)),
                  pl.BlockSpec((ka + kb, tn), lambda i, j: (0, j)),
                  pl.BlockSpec((tm, tn), lambda i, j: (i, j))],
        out_specs=pl.BlockSpec((tm, tn), lambda i, j: (i, j)),
        compiler_params=_params(("parallel", "parallel"), vmem + (8 << 20)),
        name="out_proj",
    )(oa, ob, w16, x)


HALO = SUBLANES


def _shift_rows(g, k, row, fill):
    out = pltpu.roll(g, k, 0)
    for r in range(k):
        out = jnp.where(row == r, fill(r), out)
    return out


def _ffn_kernel(*refs, seq_rows, has_state):
    if has_state:
        (h_ref, gn_ref, wg_ref, wu_ref, cw_ref, cb_ref, wd_ref, s1_ref, s2_ref,
         y_ref, gl_ref, hn_ref, acc_ref) = refs
    else:
        (h_ref, gn_ref, wg_ref, wu_ref, cw_ref, cb_ref, wd_ref,
         y_ref, gl_ref, hn_ref, acc_ref, halo_ref) = refs
    i, j = pl.program_id(0), pl.program_id(1)
    tm, tf = h_ref.shape[0], wg_ref.shape[1]

    @pl.when(j == 0)
    def _():
        x = h_ref[...]
        y = x * lax.rsqrt(jnp.mean(x * x, axis=-1, keepdims=True) + EPS)
        hn_ref[...] = (y * gn_ref[...]).astype(BF16)
        acc_ref[...] = jnp.zeros(acc_ref.shape, F32)

    hn = hn_ref[...]
    g = _dot(hn, wg_ref[...])
    u = _dot(hn, wu_ref[...])
    row = lax.broadcasted_iota(I32, (tm, tf), 0)
    if has_state:
        tpos = row % seq_rows
        gm1 = jnp.where(tpos >= 1, pltpu.roll(g, 1, 0), s1_ref[...])
        gm2 = jnp.where(tpos >= 2, pltpu.roll(g, 2, 0), s2_ref[...])
        gl_ref[...] = g
    else:
        tiles_per_seq = seq_rows // tm

        @pl.when(i % tiles_per_seq == 0)
        def _():
            halo_ref[j] = jnp.zeros((HALO, tf), F32)

        prev = halo_ref[j]
        gm1 = _shift_rows(g, 1, row, lambda r: prev[HALO - 1 + r:HALO + r, :])
        gm2 = _shift_rows(g, 2, row, lambda r: prev[HALO - 2 + r:HALO - 1 + r, :])
        halo_ref[j] = g[tm - HALO:, :]
        gl_ref[0] = g[tm - HALO:, :]
    gc = cb_ref[...] + cw_ref[0:1, :] * gm2 + cw_ref[1:2, :] * gm1 + cw_ref[2:3, :] * g
    act = gc * (1.0 / (1.0 + jnp.exp(-gc))) * u
    acc_ref[...] += _dot(act.astype(BF16), wd_ref[...])

    @pl.when(j == pl.num_programs(1) - 1)
    def _():
        y_ref[...] = h_ref[...] + acc_ref[...]


def _ffn_call(h, g_ffn, wg16, wu16, cw, cb, wd16, tm, tf, seq_rows, state=None):
    n, d = h.shape
    dff = wg16.shape[1]
    nj = dff // tf
    has_state = state is not None
    in_specs = [pl.BlockSpec((tm, d), lambda i, j: (i, 0)),
                pl.BlockSpec((1, d), lambda i, j: (0, 0)),
                pl.BlockSpec((d, tf), lambda i, j: (0, j)),
                pl.BlockSpec((d, tf), lambda i, j: (0, j)),
                pl.BlockSpec((CONV_W, tf), lambda i, j: (0, j)),
                pl.BlockSpec((1, tf), lambda i, j: (0, j)),
                pl.BlockSpec((tf, d), lambda i, j: (j, 0))]
    args = [h, g_ffn.reshape(1, d), wg16, wu16, cw, cb.reshape(1, dff), wd16]
    scratch = [pltpu.VMEM((tm, d), BF16), pltpu.VMEM((tm, d), F32)]
    if has_state:
        in_specs += [pl.BlockSpec((tm, tf), lambda i, j: (i, j))] * 2
        args += list(state)
        gl_shape = jax.ShapeDtypeStruct((n, dff), F32)
        gl_spec = pl.BlockSpec((tm, tf), lambda i, j: (i, j))
    else:
        assert seq_rows % tm == 0
        gl_shape = jax.ShapeDtypeStruct((n // tm, HALO, dff), F32)
        gl_spec = pl.BlockSpec((1, HALO, tf), lambda i, j: (i, 0, j))
        scratch.append(pltpu.VMEM((nj, HALO, tf), F32))
    vmem = 2 * (2 * _nbytes((tm, d), F32) + 3 * _nbytes((d, tf), BF16)) + _nbytes((tm, d), BF16) \
        + _nbytes((tm, d), F32) + 6 * _nbytes((tm, tf), F32)
    return pl.pallas_call(
        functools.partial(_ffn_kernel, seq_rows=seq_rows, has_state=has_state),
        out_shape=[jax.ShapeDtypeStruct((n, d), F32), gl_shape],
        grid=(n // tm, nj),
        in_specs=in_specs,
        out_specs=[pl.BlockSpec((tm, d), lambda i, j: (i, 0)), gl_spec],
        scratch_shapes=scratch,
        compiler_params=_params(("arbitrary", "arbitrary"), vmem + (8 << 20)),
        name="conv_ffn_state" if has_state else "conv_ffn",
    )(*args)


def _rope_tables(pos, dim):
    half = dim // 2
    inv = ROPE_THETA ** (-jnp.arange(half, dtype=F32) / half)
    ang = pos.astype(F32)[:, None] * inv[None, :]
    cos = jnp.tile(jnp.concatenate([jnp.cos(ang), jnp.cos(ang)], axis=-1), (1, LANES // dim))
    sin = jnp.tile(jnp.concatenate([-jnp.sin(ang), jnp.sin(ang)], axis=-1), (1, LANES // dim))
    return cos, sin


def _pad_to(x, size, axis):
    pad = [(0, 0)] * x.ndim
    pad[axis] = (0, size - x.shape[axis])
    return jnp.pad(x, pad)


def _project(x2d, wts, tables, tm, rope_period):
    xn = _rms_call(x2d, wts["g_attn"], tm)
    w16 = wts["w_main"]
    (cos_h, sin_h), (cos_i, sin_i) = tables
    hw = H_A * HEAD_DIM
    kw = KV_B * HEAD_DIM
    call = functools.partial(_proj_call, xn, w16, tm=tm)
    (qa16,) = call(0, hw, tn=512, gain=jnp.tile(wts["g_qa"], H_A), want32=False, want16=True, name="proj_qa")
    ka32, ka16 = call(hw, hw, tn=512, gain=jnp.tile(wts["g_ka"], H_A), want32=True, want16=True, name="proj_ka")
    va32, va16 = call(2 * hw, hw, tn=512, want32=True, want16=True, name="proj_va")
    (qb16,) = call(3 * hw, hw, tn=512, gain=jnp.tile(wts["g_qb"], H_B), rope=(cos_h, sin_h, HEAD_DIM),
                   rope_period=rope_period, want32=False, want16=True, name="proj_qb")
    kb32, kb16 = call(4 * hw, kw, tn=kw, gain=jnp.tile(wts["g_kb"], KV_B), rope=(cos_h, sin_h, HEAD_DIM),
                      rope_period=rope_period, want32=True, want16=True, name="proj_kb")
    vb32, vb16 = call(4 * hw + kw, kw, tn=kw, want32=True, want16=True, name="proj_vb")
    (qi16,) = call(4 * hw + 2 * kw, H_IDX * D_IDX, tn=512, rope=(cos_i, sin_i, D_IDX),
                   rope_period=rope_period, want32=False, want16=True, name="proj_qi")
    ki32, ki16, logf, wi = _misc_call(xn, wts["w_misc"], wts["b_f"], cos_i, sin_i, tm, rope_period)
    return dict(qa16=qa16, ka32=ka32, ka16=ka16, va32=va32, va16=va16, qb16=qb16, kb32=kb32, kb16=kb16,
                vb32=vb32, vb16=vb16, qi16=qi16, ki32=ki32, ki16=ki16, logf=logf, wi=wi)


def _layer_weights(l, w_in, b_f, g_qa, g_ka, g_qb, g_kb, g_attn, w_out, g_ffn, w_gate, w_up, conv_w, conv_b, w_down):
    hw, kw = H_A * HEAD_DIM, KV_B * HEAD_DIM
    splits = (hw, hw, hw, H_A, H_B * HEAD_DIM, kw, kw, H_IDX * D_IDX, D_IDX, H_IDX)
    pts = [0] + [int(v) for v in np.cumsum(splits)]
    col = lambda k: w_in[l][:, pts[k]:pts[k + 1]]
    w_main = jnp.concatenate([col(0), col(1), col(2), col(4), col(5), col(6), col(7)], axis=1).astype(BF16)
    w_misc = jnp.concatenate([_pad_to(col(8), LANES, 1), _pad_to(col(3), LANES, 1), _pad_to(col(9), LANES, 1)],
                             axis=1).astype(BF16)
    dff = w_gate.shape[2]
    dffp = -(-dff // 512) * 512
    return dict(
        w_main=w_main, w_misc=w_misc, b_f=b_f[l], g_qa=g_qa[l], g_ka=g_ka[l], g_qb=g_qb[l], g_kb=g_kb[l],
        g_attn=g_attn[l], w_out=w_out[l].astype(BF16), g_ffn=g_ffn[l],
        w_gate=_pad_to(w_gate[l], dffp, 1).astype(BF16), w_up=_pad_to(w_up[l], dffp, 1).astype(BF16),
        conv_w=_pad_to(conv_w[l], dffp, 1), conv_b=_pad_to(conv_b[l], dffp, 0),
        w_down=_pad_to(w_down[l], dffp, 0).astype(BF16), dff=dff)


def _prompt_layer(x, wts, tables):
    nb, t, d = x.shape
    x2d = x.reshape(nb * t, d)
    pr = _project(x2d, wts, tables, tm=512, rope_period=t)
    logf_t = jnp.swapaxes(pr["logf"].reshape(nb, t, H_A), 1, 2)
    c_lanes = _cumsum_call(logf_t)
    c_rows = jnp.swapaxes(c_lanes, 1, 2).reshape(nb * t, H_A)
    oa = _fox_call(pr["qa16"], pr["ka16"], pr["va16"], c_rows, c_lanes, nb, t, tq=512)
    n_sel = min(TOPK_MAX, t // 4)
    sc, thr, jcut = _idx_call(pr["qi16"], pr["wi"], pr["ki16"], nb, t, tq=256, n_sel=n_sel)
    ob = _dsa_call(pr["qb16"], pr["kb16"], pr["vb16"], sc, thr, jcut, nb, t, tq=512)
    h = _outproj_call(oa, ob, wts["w_out"], x2d, tm=512, tn=512)
    y, glast = _ffn_call(h, wts["g_ffn"], wts["w_gate"], wts["w_up"], wts["conv_w"], wts["conv_b"], wts["w_down"],
                         tm=512, tf=512, seq_rows=t)
    dff = wts["dff"]
    new_g = glast.reshape(nb, t // 512, HALO, -1)[:, -1, HALO - (CONV_W - 1):, :dff]
    rows = (pr["ka32"].reshape(nb, t, H_A, HEAD_DIM), pr["va32"].reshape(nb, t, H_A, HEAD_DIM),
            pr["logf"].reshape(nb, t, H_A), pr["kb32"].reshape(nb, t, KV_B, HEAD_DIM),
            pr["vb32"].reshape(nb, t, KV_B, HEAD_DIM), pr["ki32"].reshape(nb, t, D_IDX), new_g)
    return y.reshape(nb, t, d), rows


def _sample_layer(x, wts, tables, pools, state, page_table):
    nb, nt, d = x.shape
    assert nt == SUBLANES
    n = nb * nt
    x2d = x.reshape(n, d)
    pr = _project(x2d, wts, tables, tm=n, rope_period=n)
    fox_k, fox_v, fox_lf_t, dsa_k, dsa_v, idx_k = pools
    past_len = page_table.shape[1] * PAGE_SIZE

    def heads_first(a, heads, hd):
        return jnp.swapaxes(a.reshape(nb, nt, heads, hd), 1, 2)

    def head_rows(a, heads):
        return heads_first(a, heads, HEAD_DIM).reshape(nb, heads * nt, HEAD_DIM).astype(F32)

    def new_tile(a):
        return _pad_to(a.reshape(nb, nt, -1), PAGE_SIZE, 1)

    lfnew = _pad_to(jnp.swapaxes(pr["logf"].reshape(nb, nt, H_A), 1, 2), PAGE_SIZE, 2)
    oa = _fox_decode_call(page_table, head_rows(pr["qa16"], H_A), pr["logf"], new_tile(pr["ka16"]),
                          new_tile(pr["va16"]), lfnew, fox_k, fox_v, fox_lf_t, nt)
    n_sel = min(TOPK_MAX, (past_len + nt) // 4)
    qi_rows = heads_first(pr["qi16"], H_IDX, D_IDX).reshape(nb, H_IDX * nt, D_IDX)
    wi_rows = jnp.swapaxes(pr["wi"].reshape(nb, nt, H_IDX), 1, 2).reshape(nb, H_IDX * nt, 1)
    sc, thr, jcut = _idx_decode_call(page_table, qi_rows, wi_rows, new_tile(pr["ki16"]), idx_k, nt, n_sel)
    ob = _dsa_decode_call(page_table, head_rows(pr["qb16"], H_B), new_tile(pr["kb16"]), new_tile(pr["vb16"]),
                          sc, thr, jcut, dsa_k, dsa_v, nt)
    h = _outproj_call(oa, ob, wts["w_out"], x2d, tm=n, tn=512)
    dffp = wts["w_gate"].shape[1]
    st = _pad_to(state, dffp, 2)
    zeros = jnp.zeros((nb, nt, dffp), F32)
    s1 = zeros.at[:, 0].set(st[:, 1]).reshape(n, dffp)
    s2 = zeros.at[:, 0].set(st[:, 0]).at[:, 1].set(st[:, 1]).reshape(n, dffp)
    y, gfull = _ffn_call(h, wts["g_ffn"], wts["w_gate"], wts["w_up"], wts["conv_w"], wts["conv_b"], wts["w_down"],
                         tm=n, tf=512, seq_rows=nt, state=(s1, s2))
    dff = wts["dff"]
    new_g = gfull.reshape(nb, nt, dffp)[:, nt - (CONV_W - 1):, :dff]
    rows = (pr["ka32"].reshape(nb, nt, H_A, HEAD_DIM), pr["va32"].reshape(nb, nt, H_A, HEAD_DIM),
            pr["logf"].reshape(nb, nt, H_A), pr["kb32"].reshape(nb, nt, KV_B, HEAD_DIM),
            pr["vb32"].reshape(nb, nt, KV_B, HEAD_DIM), pr["ki32"].reshape(nb, nt, D_IDX), new_g)
    return y.reshape(nb, nt, d), rows


def kernel(x_prompt, x_sample, cache_fox_k, cache_fox_v, cache_fox_logf, cache_dsa_k, cache_dsa_v, cache_idx_k,
           state_ffn_conv, page_table, w_in, b_f, g_qa, g_ka, g_qb, g_kb, g_attn, w_out, g_ffn, w_gate, w_up,
           conv_w, conv_b, w_down):
    depth = w_in.shape[0]
    t = x_prompt.shape[1]
    nb_s, nt_s = x_sample.shape[:2]
    past_len = page_table.shape[1] * PAGE_SIZE
    pos_p = jnp.arange(t, dtype=I32)
    pos_s = jnp.tile(past_len + jnp.arange(nt_s, dtype=I32), nb_s)
    tables_p = (_rope_tables(pos_p, HEAD_DIM), _rope_tables(pos_p, D_IDX))
    tables_s = (_rope_tables(pos_s, HEAD_DIM), _rope_tables(pos_s, D_IDX))
    n_pool = cache_fox_k.shape[1]
    head_pages = lambda pool: pool.reshape(depth * n_pool, PAGE_SIZE * pool.shape[3], pool.shape[4])
    flat = lambda pool: pool.reshape(depth * n_pool, PAGE_SIZE, -1)
    pools = (head_pages(cache_fox_k), head_pages(cache_fox_v), jnp.swapaxes(flat(cache_fox_logf), 1, 2),
             head_pages(cache_dsa_k), head_pages(cache_dsa_v), flat(cache_idx_k))
    y_p, y_s = x_prompt, x_sample
    rows_p, rows_s = [], []
    for l in range(depth):
        wts = _layer_weights(l, w_in, b_f, g_qa, g_ka, g_qb, g_kb, g_attn, w_out, g_ffn, w_gate, w_up,
                             conv_w, conv_b, w_down)
        y_p, new_p = _prompt_layer(y_p, wts, tables_p)
        y_s, new_s = _sample_layer(y_s, wts, tables_s, pools, state_ffn_conv[l], page_table + l * n_pool)
        rows_p.append(new_p)
        rows_s.append(new_s)
    stack = lambda vs: vs[0][None] if len(vs) == 1 else jnp.stack(vs)
    out_p = [stack(v) for v in zip(*rows_p)]
    out_s = [stack(v) for v in zip(*rows_s)]
    return (y_p, y_s, *out_p, *out_s)
```

```python
import functools

import jax
import jax.numpy as jnp
import numpy as np
from jax import lax
from jax.experimental import pallas as pl
from jax.experimental.pallas import tpu as pltpu

F32 = jnp.float32
BF16 = jnp.bfloat16
I32 = jnp.int32

LANES = 128
SUBLANES = 8
VMEM_CAP_BYTES = 56 * 1024 * 1024

HEAD_DIM = 128
H_A = 8
H_B = 8
KV_B = 2
GROUP_B = H_B // KV_B
H_IDX = 16
D_IDX = 64
TOPK_MAX = 256
PAGE_SIZE = 128
ROPE_THETA = 10000.0
EPS = 1e-6
CONV_W = 3

NEG = -1e30
INT_MIN = -(2 ** 31)
INT_MAX = 2 ** 31 - 1
NEG_INF_KEY = int(np.int32(np.uint32(0xFF800000) ^ np.uint32(0x7FFFFFFF)))

NT_DIMS = (((1,), (1,)), ((), ()))


def _nt_dot(a, b):
    return lax.dot_general(a, b, NT_DIMS, preferred_element_type=F32)


def _dot(a, b):
    return jnp.dot(a, b, preferred_element_type=F32)


def _params(semantics, vmem_bytes):
    limit = int(min(max(vmem_bytes, 16 * 1024 * 1024), VMEM_CAP_BYTES))
    return pltpu.CompilerParams(dimension_semantics=semantics, vmem_limit_bytes=limit)


def _nbytes(shape, dtype):
    return int(np.prod(shape)) * jnp.dtype(dtype).itemsize


def _f32_to_key(x):
    b = lax.bitcast_convert_type(x, I32)
    return b ^ ((b >> 31) & INT_MAX)


def _rms_kernel(x_ref, g_ref, o_ref):
    x = x_ref[...]
    y = x * lax.rsqrt(jnp.mean(x * x, axis=-1, keepdims=True) + EPS)
    o_ref[...] = (y * g_ref[...]).astype(o_ref.dtype)


def _rms_call(x, g, tm):
    n, d = x.shape
    return pl.pallas_call(
        _rms_kernel,
        out_shape=jax.ShapeDtypeStruct((n, d), BF16),
        grid=(n // tm,),
        in_specs=[pl.BlockSpec((tm, d), lambda i: (i, 0)),
                  pl.BlockSpec((1, d), lambda i: (0, 0))],
        out_specs=pl.BlockSpec((tm, d), lambda i: (i, 0)),
        compiler_params=_params(("parallel",), 2 * (_nbytes((tm, d), F32) + _nbytes((tm, d), BF16)) + (4 << 20)),
        name="rms_norm",
    )(x, g.reshape(1, d))


PROJ_SUB = 256


def _rope_lanes(y, cos, sin, dim):
    if dim == LANES:
        rot = pltpu.roll(y, LANES // 2, 1)
    else:
        lane = lax.broadcasted_iota(I32, y.shape, 1)
        first_half = (lane % dim) < (dim // 2)
        rot = jnp.where(first_half, pltpu.roll(y, LANES - dim // 2, 1), pltpu.roll(y, dim // 2, 1))
    return y * cos + rot * sin


def _proj_kernel(*refs, has_gain, rope_dim, n32, n16):
    refs = list(refs)
    xn_ref, w_ref = refs[0], refs[1]
    pos = 2
    gain_ref = cos_ref = sin_ref = None
    if has_gain:
        gain_ref = refs[pos]
        pos += 1
    if rope_dim:
        cos_ref, sin_ref = refs[pos], refs[pos + 1]
        pos += 2
    out_refs = refs[pos:pos + n32 + n16]
    tn = w_ref.shape[1]
    sub = min(tn, PROJ_SUB)
    xn = xn_ref[...]
    accs = [_dot(xn, w_ref[:, c0:c0 + sub]) for c0 in range(0, tn, sub)]
    for c in range(tn // LANES):
        cs = slice(c * LANES, (c + 1) * LANES)
        c_in = (c * LANES) % sub
        y = accs[(c * LANES) // sub][:, c_in:c_in + LANES]
        if has_gain:
            y = y * lax.rsqrt(jnp.mean(y * y, axis=-1, keepdims=True) + EPS) * gain_ref[:, cs]
        if rope_dim:
            y = _rope_lanes(y, cos_ref[...], sin_ref[...], rope_dim)
        for o in out_refs:
            o[:, cs] = y.astype(o.dtype)


def _proj_call(xn, w16, col0, ncols, tm, tn, *, gain=None, rope=None, rope_period=None, want32, want16, name):
    n, d = xn.shape
    assert col0 % tn == 0 and ncols % tn == 0 and n % tm == 0
    cb0 = col0 // tn
    in_specs = [pl.BlockSpec((tm, d), lambda i, j: (i, 0)),
                pl.BlockSpec((d, tn), lambda i, j: (0, cb0 + j))]
    args = [xn, w16]
    if gain is not None:
        in_specs.append(pl.BlockSpec((1, tn), lambda i, j: (0, j)))
        args.append(gain.reshape(1, ncols))
    rope_dim = 0
    if rope is not None:
        cos, sin, rope_dim = rope
        nper = rope_period // tm
        spec = pl.BlockSpec((tm, LANES), lambda i, j: (i % nper, 0))
        in_specs += [spec, spec]
        args += [cos, sin]
    out_shape, out_specs = [], []
    for dt, want in ((F32, want32), (BF16, want16)):
        if want:
            out_shape.append(jax.ShapeDtypeStruct((n, ncols), dt))
            out_specs.append(pl.BlockSpec((tm, tn), lambda i, j: (i, j)))
    vmem = 2 * (_nbytes((tm, d), BF16) + _nbytes((d, tn), BF16) + 2 * _nbytes((tm, tn), F32)) + 4 * _nbytes((tm, tn), F32)
    outs = pl.pallas_call(
        functools.partial(_proj_kernel, has_gain=gain is not None, rope_dim=rope_dim,
                          n32=int(want32), n16=int(want16)),
        out_shape=out_shape,
        grid=(n // tm, ncols // tn),
        in_specs=in_specs,
        out_specs=out_specs,
        compiler_params=_params(("parallel", "parallel"), vmem + (8 << 20)),
        name=name,
    )(*args)
    return outs


def _misc_kernel(xn_ref, w_ref, bf_ref, cos_ref, sin_ref, ki32_ref, ki16_ref, logf_ref, wi_ref):
    acc = _dot(xn_ref[...], w_ref[...])
    ki = _rope_lanes(acc[:, :LANES], cos_ref[...], sin_ref[...], D_IDX)[:, :D_IDX]
    ki32_ref[...] = ki
    ki16_ref[...] = ki.astype(BF16)
    z = acc[:, LANES:LANES + H_A] + bf_ref[...]
    logf_ref[...] = jnp.minimum(z, 0.0) - jnp.log1p(jnp.exp(-jnp.abs(z)))
    wi_ref[...] = acc[:, 2 * LANES:2 * LANES + H_IDX]


def _misc_call(xn, w_misc, b_f, cos, sin, tm, rope_period):
    n, d = xn.shape
    nper = rope_period // tm
    tspec = pl.BlockSpec((tm, LANES), lambda i: (i % nper, 0))
    return pl.pallas_call(
        _misc_kernel,
        out_shape=[jax.ShapeDtypeStruct((n, D_IDX), F32), jax.ShapeDtypeStruct((n, D_IDX), BF16),
                   jax.ShapeDtypeStruct((n, H_A), F32), jax.ShapeDtypeStruct((n, H_IDX), F32)],
        grid=(n // tm,),
        in_specs=[pl.BlockSpec((tm, d), lambda i: (i, 0)),
                  pl.BlockSpec((d, 3 * LANES), lambda i: (0, 0)),
                  pl.BlockSpec((1, H_A), lambda i: (0, 0)),
                  tspec, tspec],
        out_specs=[pl.BlockSpec((tm, D_IDX), lambda i: (i, 0)), pl.BlockSpec((tm, D_IDX), lambda i: (i, 0)),
                   pl.BlockSpec((tm, H_A), lambda i: (i, 0)), pl.BlockSpec((tm, H_IDX), lambda i: (i, 0))],
        compiler_params=_params(("parallel",), 32 << 20),
        name="proj_misc",
    )(xn, w_misc, b_f.reshape(1, H_A), cos, sin)


def _lane_prefix(x):
    lane = lax.broadcasted_iota(I32, x.shape, 1)
    k = 1
    while k < LANES:
        x = x + jnp.where(lane >= k, pltpu.roll(x, k, 1), 0.0)
        k *= 2
    return x


def _lane_suffix(x):
    lane = lax.broadcasted_iota(I32, x.shape, 1)
    k = 1
    while k < LANES:
        x = x + jnp.where(lane < LANES - k, pltpu.roll(x, LANES - k, 1), 0.0)
        k *= 2
    return x


def _cumsum_kernel(x_ref, o_ref):
    t = x_ref.shape[2]
    carry = jnp.zeros((x_ref.shape[1], 1), F32)
    for c in range(t // LANES):
        cs = slice(c * LANES, (c + 1) * LANES)
        y = _lane_prefix(x_ref[0, :, cs]) + carry
        o_ref[0, :, cs] = y
        carry = y[:, LANES - 1:LANES]


def _cumsum_call(logf_t):
    b, h, t = logf_t.shape
    return pl.pallas_call(
        _cumsum_kernel,
        out_shape=jax.ShapeDtypeStruct((b, h, t), F32),
        grid=(b,),
        in_specs=[pl.BlockSpec((1, h, t), lambda i: (i, 0, 0))],
        out_specs=pl.BlockSpec((1, h, t), lambda i: (i, 0, 0)),
        compiler_params=_params(("parallel",), 16 << 20),
        name="logf_cumsum",
    )(logf_t)


MASKED = 2 * NEG


def _softmax_step(s, h, hs, v, m_ref, l_ref, acc_ref):
    m_prev = m_ref[h]
    m_new = jnp.maximum(m_prev, jnp.max(s, axis=-1, keepdims=True))
    alpha = jnp.exp(m_prev - m_new)
    p = jnp.exp(s - m_new)
    l_ref[h] = alpha * l_ref[h] + jnp.sum(p, axis=-1, keepdims=True)
    acc_ref[:, hs] = alpha * acc_ref[:, hs] + _dot(p.astype(BF16), v)
    m_ref[h] = m_new


def _init_softmax(m_ref, l_ref, acc_ref):
    m_ref[...] = jnp.full(m_ref.shape, NEG, F32)
    l_ref[...] = jnp.zeros(l_ref.shape, F32)
    acc_ref[...] = jnp.zeros(acc_ref.shape, F32)


def _fox_kernel(q_ref, k_ref, v_ref, cq_ref, ck_ref, o_ref, m_ref, l_ref, acc_ref, *, scale):
    i, j = pl.program_id(1), pl.program_id(2)
    tq, tk = q_ref.shape[0], k_ref.shape[0]

    @pl.when(j == 0)
    def _():
        _init_softmax(m_ref, l_ref, acc_ref)

    def step(diag):
        causal = None
        if diag:
            causal = lax.broadcasted_iota(I32, (tq, tk), 0) >= lax.broadcasted_iota(I32, (tq, tk), 1)

        qk = lambda h: _nt_dot(q_ref[:, h * HEAD_DIM:(h + 1) * HEAD_DIM], k_ref[:, h * HEAD_DIM:(h + 1) * HEAD_DIM])
        s_next = qk(0)
        for h in range(H_A):
            hs = slice(h * HEAD_DIM, (h + 1) * HEAD_DIM)
            s = s_next * scale
            if h + 1 < H_A:
                s_next = qk(h + 1)
            s = s + cq_ref[:, h:h + 1] - ck_ref[0, h:h + 1, :]
            if diag:
                s = jnp.where(causal, s, MASKED)
            _softmax_step(s, h, hs, v_ref[:, hs], m_ref, l_ref, acc_ref)

    pl.when(j < i)(functools.partial(step, False))

    @pl.when(j == i)
    def _():
        step(True)
        for h in range(H_A):
            hs = slice(h * HEAD_DIM, (h + 1) * HEAD_DIM)
            o_ref[:, hs] = (acc_ref[:, hs] / l_ref[h]).astype(o_ref.dtype)


def _fox_call(q16, k16, v16, c_rows, c_lanes, nb, t, tq):
    nq = t // tq
    width = H_A * HEAD_DIM
    qmap = lambda b, i, j: (b * nq + i, 0)
    kmap = lambda b, i, j: (b * nq + jnp.minimum(j, i), 0)
    vmem = 2 * 4 * _nbytes((tq, width), BF16) + _nbytes((tq, width), F32) + 2 * H_A * _nbytes((tq, LANES), F32)
    return pl.pallas_call(
        functools.partial(_fox_kernel, scale=HEAD_DIM ** -0.5),
        out_shape=jax.ShapeDtypeStruct((nb * t, width), BF16),
        grid=(nb, nq, nq),
        in_specs=[pl.BlockSpec((tq, width), qmap),
                  pl.BlockSpec((tq, width), kmap),
                  pl.BlockSpec((tq, width), kmap),
                  pl.BlockSpec((tq, H_A), qmap),
                  pl.BlockSpec((1, H_A, tq), lambda b, i, j: (b, 0, jnp.minimum(j, i)))],
        out_specs=pl.BlockSpec((tq, width), qmap),
        scratch_shapes=[pltpu.VMEM((H_A, tq, 1), F32), pltpu.VMEM((H_A, tq, 1), F32),
                        pltpu.VMEM((tq, width), F32)],
        compiler_params=_params(("parallel", "parallel", "arbitrary"), vmem + (16 << 20)),
        name="fox_prompt_attention",
    )(q16, k16, v16, c_rows, c_lanes)


SELECT_CHUNK = 256
SELECT_LANES = 256
SELECT_ACCS = 4


def _select_kernel(kt_ref, thr_ref, jcut_ref, *, n_sel, keys_per_step):
    nq = kt_ref.shape[2]
    if keys_per_step is None:
        n_chunks = kt_ref.shape[1] // SELECT_CHUNK
    else:
        n_chunks = (pl.program_id(1) + 1) * (keys_per_step // SELECT_CHUNK)
    nf = float(n_sel)

    def count(row_args, pred):
        args = [jnp.broadcast_to(a, (SUBLANES, nq)) for a in row_args]

        def body(c, accs):
            accs = list(accs)
            start = pl.multiple_of(c * SELECT_CHUNK, SELECT_CHUNK)
            blk = kt_ref[0, pl.ds(start, SELECT_CHUNK), :]
            for r in range(SELECT_CHUNK // SUBLANES):
                pos = start + r * SUBLANES + lax.broadcasted_iota(I32, (SUBLANES, nq), 0)
                a = accs[r % SELECT_ACCS]
                accs[r % SELECT_ACCS] = jnp.where(pred(blk[r * SUBLANES:(r + 1) * SUBLANES, :], pos, *args), a + 1.0, a)
            return tuple(accs)

        zero = jnp.zeros((SUBLANES, nq), F32)
        accs = lax.fori_loop(0, n_chunks, body, (zero,) * SELECT_ACCS)
        return jnp.sum(functools.reduce(lambda a, b: a + b, accs), axis=0, keepdims=True)

    count_ge = lambda cand: count((cand,), lambda k, pos, c: k >= c)
    base = jnp.where(count_ge(jnp.zeros((1, nq), I32)) >= nf, 0, INT_MIN).astype(I32)

    def bit_body(b, base):
        cand = base | jnp.left_shift(jnp.int32(1), 30 - b)
        return jnp.where(count_ge(cand) >= nf, cand, base)

    base = lax.fori_loop(0, 31, bit_body, base)
    n_ge = count_ge(base)
    n_gt = count_ge(base + 1)
    need = jnp.logical_and(n_ge > nf, base != NEG_INF_KEY)
    thr_ref[...] = base.reshape(thr_ref.shape)
    jcut_ref[...] = jnp.full(jcut_ref.shape, INT_MAX, I32)

    @pl.when(jnp.max(jnp.where(need, 1.0, 0.0)) > 0.0)
    def _():
        want = nf - n_gt
        count_eq_below = lambda x: count((base, x), lambda k, pos, b, xx: jnp.logical_and(k == b, pos < xx))

        def pos_body(b, x):
            cand = x | jnp.left_shift(jnp.int32(1), 14 - b)
            return jnp.where(count_eq_below(cand) < want, cand, x)

        x = lax.fori_loop(0, 15, pos_body, jnp.zeros((1, nq), I32))
        jcut_ref[...] = jnp.where(need, x, INT_MAX).reshape(jcut_ref.shape)


def _select_call(keys_t, n_sel, causal_steps):
    nb, npos, nqry = keys_t.shape
    steps = nqry // SELECT_LANES
    assert steps * SELECT_LANES == nqry and npos % SELECT_CHUNK == 0 and npos < 2 ** 15
    keys_per_step = SELECT_LANES if causal_steps else None
    out = jax.ShapeDtypeStruct((nb * steps, 1, SELECT_LANES), I32)
    omap = lambda b, s: (b * steps + s, 0, 0)
    return pl.pallas_call(
        functools.partial(_select_kernel, n_sel=n_sel, keys_per_step=keys_per_step),
        out_shape=[out, out],
        grid=(nb, steps),
        in_specs=[pl.BlockSpec((1, npos, SELECT_LANES), lambda b, s: (b, 0, s))],
        out_specs=[pl.BlockSpec((1, 1, SELECT_LANES), omap), pl.BlockSpec((1, 1, SELECT_LANES), omap)],
        compiler_params=_params(("parallel", "parallel"), 2 * _nbytes((npos, SELECT_LANES), I32) + (8 << 20)),
        name="topn_threshold",
    )(keys_t)


def _canon_zero(x):
    return jnp.where(x == 0.0, 0.0, x)


def _idx_kernel(qi_ref, wi_ref, ki_ref, key_ref, keyt_ref, *, idx_scale):
    i, j = pl.program_id(1), pl.program_id(2)
    tq, tk = qi_ref.shape[0], ki_ref.shape[0]

    def scores(diag):
        acc = jnp.zeros((tq, tk), F32)
        for h in range(H_IDX):
            d = _nt_dot(qi_ref[:, h * D_IDX:(h + 1) * D_IDX], ki_ref[...])
            acc = acc + wi_ref[:, h:h + 1] * jnp.maximum(d, 0.0)
        sc = _canon_zero(acc * idx_scale)
        if diag:
            causal = lax.broadcasted_iota(I32, (tq, tk), 0) >= lax.broadcasted_iota(I32, (tq, tk), 1)
            sc = jnp.where(causal, sc, -jnp.inf)
        key = _f32_to_key(sc)
        key_ref[...] = key
        keyt_ref[0] = key.T

    pl.when(j < i)(functools.partial(scores, False))
    pl.when(j == i)(functools.partial(scores, True))

    @pl.when(j > i)
    def _():
        fill = jnp.full((tq, tk), NEG_INF_KEY, I32)
        key_ref[...] = fill
        keyt_ref[0] = fill


def _idx_call(qi16, wi, ki16, nb, t, tq):
    nq = t // tq
    qmap = lambda b, i, j: (b * nq + i, 0)
    vmem = 2 * (_nbytes((tq, H_IDX * D_IDX), BF16) + 2 * _nbytes((tq, tq), I32)) + 6 * _nbytes((tq, tq), F32)
    return pl.pallas_call(
        functools.partial(_idx_kernel, idx_scale=(H_IDX * D_IDX) ** -0.5),
        out_shape=[jax.ShapeDtypeStruct((nb * t, t), I32), jax.ShapeDtypeStruct((nb, t, t), I32)],
        grid=(nb, nq, nq),
        in_specs=[pl.BlockSpec((tq, H_IDX * D_IDX), qmap),
                  pl.BlockSpec((tq, H_IDX), qmap),
                  pl.BlockSpec((tq, D_IDX), lambda b, i, j: (b * nq + jnp.minimum(j, i), 0))],
        out_specs=[pl.BlockSpec((tq, tq), lambda b, i, j: (b * nq + i, j)),
                   pl.BlockSpec((1, tq, tq), lambda b, i, j: (b, j, i))],
        compiler_params=_params(("parallel", "parallel", "parallel"), vmem + (8 << 20)),
        name="dsa_prompt_indexer",
    )(qi16, wi, ki16)


def _selected(key, thr, jcut, col):
    return jnp.logical_or(key > thr, jnp.logical_and(key == thr, col <= jcut))


def _dsa_kernel(q_ref, k_ref, v_ref, sc_ref, thr_ref, jcut_ref, o_ref, m_ref, l_ref, acc_ref, mb_ref, *, scale):
    i, j = pl.program_id(1), pl.program_id(2)
    tq, tk = q_ref.shape[0], k_ref.shape[0]

    @pl.when(j == 0)
    def _():
        _init_softmax(m_ref, l_ref, acc_ref)

    @pl.when(j <= i)
    def _():
        kv = lambda h: slice((h // GROUP_B) * HEAD_DIM, (h // GROUP_B + 1) * HEAD_DIM)
        row = i * tq + lax.broadcasted_iota(I32, (tq, tk), 0)
        col = j * tk + lax.broadcasted_iota(I32, (tq, tk), 1)
        mask = jnp.logical_and(row >= col, _selected(sc_ref[...], thr_ref[...], jcut_ref[...], col))
        mb_ref[...] = jnp.where(mask, 0.0, MASKED)
        logits = lambda h: _nt_dot(q_ref[:, h * HEAD_DIM:(h + 1) * HEAD_DIM], k_ref[:, kv(h)]) * scale + mb_ref[...]
        s_next = logits(0)
        for h in range(H_B):
            hs = slice(h * HEAD_DIM, (h + 1) * HEAD_DIM)
            s = s_next
            if h + 1 < H_B:
                s_next = logits(h + 1)
            _softmax_step(s, h, hs, v_ref[:, kv(h)], m_ref, l_ref, acc_ref)

    @pl.when(j == i)
    def _():
        for h in range(H_B):
            hs = slice(h * HEAD_DIM, (h + 1) * HEAD_DIM)
            o_ref[:, hs] = (acc_ref[:, hs] / l_ref[h]).astype(o_ref.dtype)


def _dsa_call(q16, k16, v16, sc, thr, jcut, nb, t, tq):
    nq = t // tq
    qw, kw = H_B * HEAD_DIM, KV_B * HEAD_DIM
    qmap = lambda b, i, j: (b * nq + i, 0)
    kmap = lambda b, i, j: (b * nq + jnp.minimum(j, i), 0)
    vmem = 2 * (2 * _nbytes((tq, qw), BF16) + 2 * _nbytes((tq, kw), BF16) + _nbytes((tq, tq), F32)) \
        + _nbytes((tq, qw), F32) + 2 * H_B * _nbytes((tq, LANES), F32)
    return pl.pallas_call(
        functools.partial(_dsa_kernel, scale=HEAD_DIM ** -0.5),
        out_shape=jax.ShapeDtypeStruct((nb * t, qw), BF16),
        grid=(nb, nq, nq),
        in_specs=[pl.BlockSpec((tq, qw), qmap),
                  pl.BlockSpec((tq, kw), kmap),
                  pl.BlockSpec((tq, kw), kmap),
                  pl.BlockSpec((tq, tq), lambda b, i, j: (b * nq + i, jnp.minimum(j, i))),
                  pl.BlockSpec((tq, 1), qmap),
                  pl.BlockSpec((tq, 1), qmap)],
        out_specs=pl.BlockSpec((tq, qw), qmap),
        scratch_shapes=[pltpu.VMEM((H_B, tq, 1), F32), pltpu.VMEM((H_B, tq, 1), F32),
                        pltpu.VMEM((tq, qw), F32), pltpu.VMEM((tq, tq), F32)],
        compiler_params=_params(("parallel", "parallel", "arbitrary"), vmem + (16 << 20)),
        name="dsa_prompt_attention",
    )(q16, k16, v16, sc, thr, jcut)


FOX_PAGES_PER_STEP = 8
DSA_PAGES_PER_STEP = 16


def _rows_from_heads(x, reps):
    return jnp.concatenate([jnp.broadcast_to(x[h:h + 1, :], (reps, x.shape[1])) for h in range(x.shape[0])], axis=0)


def _decode_softmax_step(s, mask, pv_fn, m_ref, l_ref, acc_ref):
    if mask is not None:
        s = jnp.where(mask, s, NEG)
    m_prev = m_ref[...]
    m_new = jnp.maximum(m_prev, jnp.max(s, axis=-1, keepdims=True))
    alpha = jnp.exp(m_prev - m_new)
    p = jnp.exp(s - m_new)
    if mask is not None:
        p = jnp.where(mask, p, 0.0)
    l_ref[...] = alpha * l_ref[...] + jnp.sum(p, axis=-1, keepdims=True)
    acc_ref[...] = alpha * acc_ref[...] + pv_fn(p)
    m_ref[...] = m_new


def _group_logits(q_ref, groups, rows_per_group, k_tile_fn, n_tiles):
    out = []
    for g in range(groups):
        qg = q_ref[0, g * rows_per_group:(g + 1) * rows_per_group, :].astype(BF16)
        out.append(jnp.concatenate([_nt_dot(qg, k_tile_fn(t, g)) for t in range(n_tiles)], axis=1))
    return jnp.concatenate(out, axis=0)


def _group_pv(p, groups, rows_per_group, v_tile_fn, n_tiles):
    out = []
    for g in range(groups):
        pg = p[g * rows_per_group:(g + 1) * rows_per_group, :]
        acc = _dot(pg[:, :PAGE_SIZE].astype(BF16), v_tile_fn(0, g))
        for t in range(1, n_tiles):
            acc = acc + _dot(pg[:, t * PAGE_SIZE:(t + 1) * PAGE_SIZE].astype(BF16), v_tile_fn(t, g))
        out.append(acc)
    return jnp.concatenate(out, axis=0)


def _pool_tile(ref, g, groups):
    return ref[0, pl.ds(g, PAGE_SIZE, stride=groups), :].astype(BF16)


def _fox_decode_kernel(pt_ref, q_ref, lfq_ref, knew_ref, vnew_ref, lfnew_ref, *rest, n_tok, scale):
    del pt_ref
    npg = FOX_PAGES_PER_STEP
    k_refs, v_refs, lf_refs = rest[:npg], rest[npg:2 * npg], rest[2 * npg:3 * npg]
    o_ref, m_ref, l_ref, acc_ref, carry_ref, rowc_ref = rest[3 * npg:]
    s_id = pl.program_id(1)
    rows = H_A * n_tok
    lane_block = lambda ref, h: ref[0, :, h * HEAD_DIM:(h + 1) * HEAD_DIM]

    @pl.when(s_id == 0)
    def _():
        _init_softmax(m_ref, l_ref, acc_ref)
        lf = lfq_ref[...]
        tok = lax.broadcasted_iota(I32, lf.shape, 0)
        after = jnp.zeros(lf.shape, F32)
        for jj in range(1, n_tok):
            after = after + jnp.where(tok < jj, lf[jj:jj + 1, :], 0.0)
        rowc_ref[...] = jnp.concatenate([-after[:, h:h + 1] for h in range(H_A)], axis=0)
        lfn = lfnew_ref[0]
        incl = _lane_suffix(lfn)
        tokq = lax.broadcasted_iota(I32, (rows, LANES), 0) % n_tok
        mask = lax.broadcasted_iota(I32, (rows, LANES), 1) <= tokq
        s = _group_logits(q_ref, H_A, n_tok, lambda t, h: lane_block(knew_ref, h), 1) * scale
        s = s + rowc_ref[...] + _rows_from_heads(incl - lfn, n_tok)
        _decode_softmax_step(s, mask, lambda p: _group_pv(p, H_A, n_tok, lambda t, h: lane_block(vnew_ref, h), 1),
                             m_ref, l_ref, acc_ref)
        carry_ref[...] = incl[:, 0:1]

    @pl.when(s_id > 0)
    def _():
        carry = carry_ref[...]
        bias = [None] * npg
        for p in reversed(range(npg)):
            lf = lf_refs[p][0]
            incl = _lane_suffix(lf)
            bias[p] = _rows_from_heads(incl - lf + carry, n_tok)
            carry = carry + incl[:, 0:1]
        carry_ref[...] = carry
        s = _group_logits(q_ref, H_A, n_tok, lambda t, h: _pool_tile(k_refs[t], h, H_A), npg) * scale
        s = s + rowc_ref[...] + jnp.concatenate(bias, axis=1)
        _decode_softmax_step(s, None, lambda p: _group_pv(p, H_A, n_tok, lambda t, h: _pool_tile(v_refs[t], h, H_A), npg),
                             m_ref, l_ref, acc_ref)

    @pl.when(s_id == pl.num_programs(1) - 1)
    def _():
        for h in range(H_A):
            rs = slice(h * n_tok, (h + 1) * n_tok)
            o_ref[:, h * HEAD_DIM:(h + 1) * HEAD_DIM] = (acc_ref[rs, :] / l_ref[rs, :]).astype(o_ref.dtype)


def _page_specs(block, npg, group_of_step):
    specs = []
    for p in range(npg):
        def imap(b, s, pt, p=p):
            return (pt[b, group_of_step(s) * npg + p],) + (0,) * (len(block) - 1)
        specs.append(pl.BlockSpec(block, imap))
    return specs


def _fox_decode_call(page_table, q_rows, lf_rows, knew, vnew, lfnew, k_pool, v_pool, lf_pool, n_tok):
    nb, n_pages = page_table.shape
    npg = FOX_PAGES_PER_STEP
    n_groups = n_pages // npg
    assert n_groups * npg == n_pages
    rows, width = H_A * n_tok, H_A * HEAD_DIM
    bmap = lambda b, s, pt: (b, 0, 0)
    mirrored = lambda s: n_groups - 1 - jnp.maximum(s - 1, 0)
    page_rows = PAGE_SIZE * H_A
    in_specs = [pl.BlockSpec((1, rows, HEAD_DIM), bmap),
                pl.BlockSpec((n_tok, H_A), lambda b, s, pt: (b, 0)),
                pl.BlockSpec((1, PAGE_SIZE, width), bmap),
                pl.BlockSpec((1, PAGE_SIZE, width), bmap),
                pl.BlockSpec((1, H_A, PAGE_SIZE), bmap)]
    in_specs += _page_specs((1, page_rows, HEAD_DIM), npg, mirrored)
    in_specs += _page_specs((1, page_rows, HEAD_DIM), npg, mirrored)
    in_specs += _page_specs((1, H_A, PAGE_SIZE), npg, mirrored)
    vmem = 2 * 2 * npg * _nbytes((page_rows, HEAD_DIM), F32) + 3 * npg * _nbytes((page_rows, HEAD_DIM), BF16) \
        + 8 * _nbytes((rows, npg * PAGE_SIZE), F32)
    return pl.pallas_call(
        functools.partial(_fox_decode_kernel, n_tok=n_tok, scale=HEAD_DIM ** -0.5),
        out_shape=jax.ShapeDtypeStruct((nb * n_tok, width), BF16),
        grid_spec=pltpu.PrefetchScalarGridSpec(
            num_scalar_prefetch=1,
            grid=(nb, n_groups + 1),
            in_specs=in_specs,
            out_specs=pl.BlockSpec((n_tok, width), lambda b, s, pt: (b, 0)),
            scratch_shapes=[pltpu.VMEM((rows, 1), F32), pltpu.VMEM((rows, 1), F32),
                            pltpu.VMEM((rows, HEAD_DIM), F32), pltpu.VMEM((H_A, 1), F32),
                            pltpu.VMEM((rows, 1), F32)]),
        compiler_params=_params(("parallel", "arbitrary"), vmem + (8 << 20)),
        name="fox_decode_attention",
    )(page_table, q_rows, lf_rows, knew, vnew, lfnew, *([k_pool] * npg), *([v_pool] * npg), *([lf_pool] * npg))


def _idx_decode_kernel(pt_ref, qi_ref, wi_ref, kinew_ref, *rest, n_tok, idx_scale):
    del pt_ref
    npg = DSA_PAGES_PER_STEP
    ki_refs = rest[:npg]
    sc_ref = rest[npg]
    s_id = pl.program_id(1)
    last = pl.num_programs(1) - 1

    def page_scores(ki16):
        d = jnp.maximum(_nt_dot(qi_ref[0], ki16), 0.0) * wi_ref[0]
        acc = jnp.zeros((n_tok, PAGE_SIZE), F32)
        for h in range(H_IDX):
            acc = acc + d[h * n_tok:(h + 1) * n_tok, :]
        return _canon_zero(acc * idx_scale)

    def put(p, key):
        sc_ref[0, :, p * PAGE_SIZE:(p + 1) * PAGE_SIZE] = key

    @pl.when(s_id < last)
    def _():
        for p in range(npg):
            put(p, _f32_to_key(page_scores(ki_refs[p][0].astype(BF16))))

    @pl.when(s_id == last)
    def _():
        sc = page_scores(kinew_ref[0])
        causal = lax.broadcasted_iota(I32, sc.shape, 1) <= lax.broadcasted_iota(I32, sc.shape, 0)
        put(0, _f32_to_key(jnp.where(causal, sc, -jnp.inf)))
        for p in range(1, npg):
            put(p, jnp.full((n_tok, PAGE_SIZE), NEG_INF_KEY, I32))


def _idx_decode_call(page_table, qi_rows, wi_rows, kinew, ki_pool, n_tok):
    nb, n_pages = page_table.shape
    npg = DSA_PAGES_PER_STEP
    n_groups = n_pages // npg
    assert n_groups * npg == n_pages
    width = npg * PAGE_SIZE
    n_cols = (n_groups + 1) * width
    bmap = lambda b, s, pt: (b, 0, 0)
    forward = lambda s: jnp.minimum(s, n_groups - 1)
    in_specs = [pl.BlockSpec((1, H_IDX * n_tok, D_IDX), bmap),
                pl.BlockSpec((1, H_IDX * n_tok, 1), bmap),
                pl.BlockSpec((1, PAGE_SIZE, D_IDX), bmap)]
    in_specs += _page_specs((1, PAGE_SIZE, D_IDX), npg, forward)
    return pl.pallas_call(
        functools.partial(_idx_decode_kernel, n_tok=n_tok, idx_scale=(H_IDX * D_IDX) ** -0.5),
        out_shape=jax.ShapeDtypeStruct((nb, n_tok, n_cols), I32),
        grid_spec=pltpu.PrefetchScalarGridSpec(
            num_scalar_prefetch=1,
            grid=(nb, n_groups + 1),
            in_specs=in_specs,
            out_specs=pl.BlockSpec((1, n_tok, width), lambda b, s, pt: (b, 0, s))),
        compiler_params=_params(("parallel", "parallel"), 24 << 20),
        name="dsa_decode_indexer",
    )(page_table, qi_rows, wi_rows, kinew, *([ki_pool] * npg))


def _dsa_decode_kernel(pt_ref, q_ref, knew_ref, vnew_ref, sc_ref, thr_ref, jcut_ref, *rest, n_tok, scale):
    del pt_ref
    npg = DSA_PAGES_PER_STEP
    k_refs, v_refs = rest[:npg], rest[npg:2 * npg]
    o_ref, m_ref, l_ref, acc_ref = rest[2 * npg:]
    s_id = pl.program_id(1)
    last = pl.num_programs(1) - 1
    width = npg * PAGE_SIZE
    grows = GROUP_B * n_tok
    lane_block = lambda ref, g: ref[0, :, g * HEAD_DIM:(g + 1) * HEAD_DIM]

    @pl.when(s_id == 0)
    def _():
        _init_softmax(m_ref, l_ref, acc_ref)

    def tile_mask(p, extra):
        col = s_id * width + p * PAGE_SIZE + lax.broadcasted_iota(I32, (n_tok, PAGE_SIZE), 1)
        sel = _selected(sc_ref[0, :, p * PAGE_SIZE:(p + 1) * PAGE_SIZE], thr_ref[0], jcut_ref[0], col)
        if extra is not None:
            sel = jnp.logical_and(sel, extra)
        return jnp.concatenate([jnp.where(sel, 1.0, 0.0)] * H_B, axis=0)

    @pl.when(s_id < last)
    def _():
        s = _group_logits(q_ref, KV_B, grows, lambda t, g: _pool_tile(k_refs[t], g, KV_B), npg) * scale
        mask = jnp.concatenate([tile_mask(p, None) for p in range(npg)], axis=1) > 0.5
        _decode_softmax_step(s, mask, lambda p: _group_pv(p, KV_B, grows, lambda t, g: _pool_tile(v_refs[t], g, KV_B), npg),
                             m_ref, l_ref, acc_ref)

    @pl.when(s_id == last)
    def _():
        causal = lax.broadcasted_iota(I32, (n_tok, PAGE_SIZE), 1) <= lax.broadcasted_iota(I32, (n_tok, PAGE_SIZE), 0)
        s = _group_logits(q_ref, KV_B, grows, lambda t, g: lane_block(knew_ref, g), 1) * scale
        _decode_softmax_step(s, tile_mask(0, causal) > 0.5,
                             lambda p: _group_pv(p, KV_B, grows, lambda t, g: lane_block(vnew_ref, g), 1),
                             m_ref, l_ref, acc_ref)
        for h in range(H_B):
            rs = slice(h * n_tok, (h + 1) * n_tok)
            o_ref[:, h * HEAD_DIM:(h + 1) * HEAD_DIM] = (acc_ref[rs, :] / l_ref[rs, :]).astype(o_ref.dtype)


def _dsa_decode_call(page_table, q_rows, knew, vnew, sc, thr, jcut, k_pool, v_pool, n_tok):
    nb, n_pages = page_table.shape
    npg = DSA_PAGES_PER_STEP
    n_groups = n_pages // npg
    assert n_groups * npg == n_pages
    width = npg * PAGE_SIZE
    rows, kw = H_B * n_tok, KV_B * HEAD_DIM
    page_rows = PAGE_SIZE * KV_B
    bmap = lambda b, s, pt: (b, 0, 0)
    forward = lambda s: jnp.minimum(s, n_groups - 1)
    in_specs = [pl.BlockSpec((1, rows, HEAD_DIM), bmap),
                pl.BlockSpec((1, PAGE_SIZE, kw), bmap),
                pl.BlockSpec((1, PAGE_SIZE, kw), bmap),
                pl.BlockSpec((1, n_tok, width), lambda b, s, pt: (b, 0, s)),
                pl.BlockSpec((1, n_tok, 1), bmap),
                pl.BlockSpec((1, n_tok, 1), bmap)]
    in_specs += _page_specs((1, page_rows, HEAD_DIM), npg, forward)
    in_specs += _page_specs((1, page_rows, HEAD_DIM), npg, forward)
    return pl.pallas_call(
        functools.partial(_dsa_decode_kernel, n_tok=n_tok, scale=HEAD_DIM ** -0.5),
        out_shape=jax.ShapeDtypeStruct((nb * n_tok, H_B * HEAD_DIM), BF16),
        grid_spec=pltpu.PrefetchScalarGridSpec(
            num_scalar_prefetch=1,
            grid=(nb, n_groups + 1),
            in_specs=in_specs,
            out_specs=pl.BlockSpec((n_tok, H_B * HEAD_DIM), lambda b, s, pt: (b, 0)),
            scratch_shapes=[pltpu.VMEM((rows, 1), F32), pltpu.VMEM((rows, 1), F32),
                            pltpu.VMEM((rows, HEAD_DIM), F32)]),
        compiler_params=_params(("parallel", "arbitrary"), 32 << 20),
        name="dsa_decode_attention",
    )(page_table, q_rows, knew, vnew, sc, thr, jcut, *([k_pool] * npg), *([v_pool] * npg))


def _outproj_kernel(oa_ref, ob_ref, w_ref, x_ref, h_ref):
    ka = oa_ref.shape[1]
    h_ref[...] = x_ref[...] + _dot(oa_ref[...], w_ref[:ka, :]) + _dot(ob_ref[...], w_ref[ka:, :])


def _outproj_call(oa, ob, w16, x, tm, tn):
    n, d = x.shape
    ka, kb = oa.shape[1], ob.shape[1]
    vmem = 2 * (_nbytes((tm, ka + kb), BF16) + _nbytes((ka + kb, tn), BF16) + 2 * _nbytes((tm, tn), F32))
    return pl.pallas_call(
        _outproj_kernel,
        out_shape=jax.ShapeDtypeStruct((n, d), F32),
        grid=(n // tm, d // tn),
        in_specs=[pl.BlockSpec((tm, ka), lambda i, j: (i, 0)),
                  pl.BlockSpec((tm, kb), lambda i, j: (i, 0)),
                  pl.BlockSpec((ka + kb, tn), lambda i, j: (0, j)),
                  pl.BlockSpec((tm, tn), lambda i, j: (i, j))],
        out_specs=pl.BlockSpec((tm, tn), lambda i, j: (i, j)),
        compiler_params=_params(("parallel", "parallel"), vmem + (8 << 20)),
        name="out_proj",
    )(oa, ob, w16, x)


HALO = SUBLANES
FFN_SUB = 256


def _shift_rows(g, k, row, fill):
    out = pltpu.roll(g, k, 0)
    for r in range(k):
        out = jnp.where(row == r, fill(r), out)
    return out


def _ffn_kernel(*refs, seq_rows, has_state):
    if has_state:
        (h_ref, gn_ref, wg_ref, wu_ref, cw_ref, cb_ref, wd_ref, s1_ref, s2_ref,
         y_ref, gl_ref, hn_ref, acc_ref) = refs
    else:
        (h_ref, gn_ref, wg_ref, wu_ref, cw_ref, cb_ref, wd_ref,
         y_ref, gl_ref, hn_ref, acc_ref, halo_ref) = refs
    i, j = pl.program_id(0), pl.program_id(1)
    tm, tf = h_ref.shape[0], wg_ref.shape[1]

    @pl.when(j == 0)
    def _():
        x = h_ref[...]
        y = x * lax.rsqrt(jnp.mean(x * x, axis=-1, keepdims=True) + EPS)
        hn_ref[...] = (y * gn_ref[...]).astype(BF16)
        acc_ref[...] = jnp.zeros(acc_ref.shape, F32)

    if not has_state:
        tiles_per_seq = seq_rows // tm

        @pl.when(i % tiles_per_seq == 0)
        def _():
            halo_ref[j] = jnp.zeros((HALO, tf), F32)

    hn = hn_ref[...]
    subs = [slice(c, c + FFN_SUB) for c in range(0, tf, FFN_SUB)]
    gu = [(_dot(hn, wg_ref[:, cs]), _dot(hn, wu_ref[:, cs])) for cs in subs]
    row = lax.broadcasted_iota(I32, (tm, FFN_SUB), 0)
    down = None
    for cs, (g, u) in zip(subs, gu):
        if has_state:
            tpos = row % seq_rows
            gm1 = jnp.where(tpos >= 1, pltpu.roll(g, 1, 0), s1_ref[:, cs])
            gm2 = jnp.where(tpos >= 2, pltpu.roll(g, 2, 0), s2_ref[:, cs])
            gl_ref[:, cs] = g
        else:
            prev = halo_ref[j, :, cs]
            gm1 = _shift_rows(g, 1, row, lambda r: prev[HALO - 1 + r:HALO + r, :])
            gm2 = _shift_rows(g, 2, row, lambda r: prev[HALO - 2 + r:HALO - 1 + r, :])
            halo_ref[j, :, cs] = g[tm - HALO:, :]
            gl_ref[0, :, cs] = g[tm - HALO:, :]
        gc = cb_ref[:, cs] + cw_ref[0:1, cs] * gm2 + cw_ref[1:2, cs] * gm1 + cw_ref[2:3, cs] * g
        act = gc * (1.0 / (1.0 + jnp.exp(-gc))) * u
        d = _dot(act.astype(BF16), wd_ref[cs, :])
        down = d if down is None else down + d
    acc_ref[...] += down

    @pl.when(j == pl.num_programs(1) - 1)
    def _():
        y_ref[...] = h_ref[...] + acc_ref[...]


def _ffn_call(h, g_ffn, wg16, wu16, cw, cb, wd16, tm, tf, seq_rows, state=None):
    n, d = h.shape
    dff = wg16.shape[1]
    nj = dff // tf
    has_state = state is not None
    in_specs = [pl.BlockSpec((tm, d), lambda i, j: (i, 0)),
                pl.BlockSpec((1, d), lambda i, j: (0, 0)),
                pl.BlockSpec((d, tf), lambda i, j: (0, j)),
                pl.BlockSpec((d, tf), lambda i, j: (0, j)),
                pl.BlockSpec((CONV_W, tf), lambda i, j: (0, j)),
                pl.BlockSpec((1, tf), lambda i, j: (0, j)),
                pl.BlockSpec((tf, d), lambda i, j: (j, 0))]
    args = [h, g_ffn.reshape(1, d), wg16, wu16, cw, cb.reshape(1, dff), wd16]
    scratch = [pltpu.VMEM((tm, d), BF16), pltpu.VMEM((tm, d), F32)]
    if has_state:
        in_specs += [pl.BlockSpec((tm, tf), lambda i, j: (i, j))] * 2
        args += list(state)
        gl_shape = jax.ShapeDtypeStruct((n, dff), F32)
        gl_spec = pl.BlockSpec((tm, tf), lambda i, j: (i, j))
    else:
        assert seq_rows % tm == 0
        gl_shape = jax.ShapeDtypeStruct((n // tm, HALO, dff), F32)
        gl_spec = pl.BlockSpec((1, HALO, tf), lambda i, j: (i, 0, j))
        scratch.append(pltpu.VMEM((nj, HALO, tf), F32))
    vmem = 2 * (2 * _nbytes((tm, d), F32) + 3 * _nbytes((d, tf), BF16)) + _nbytes((tm, d), BF16) \
        + _nbytes((tm, d), F32) + 6 * _nbytes((tm, tf), F32)
    return pl.pallas_call(
        functools.partial(_ffn_kernel, seq_rows=seq_rows, has_state=has_state),
        out_shape=[jax.ShapeDtypeStruct((n, d), F32), gl_shape],
        grid=(n // tm, nj),
        in_specs=in_specs,
        out_specs=[pl.BlockSpec((tm, d), lambda i, j: (i, 0)), gl_spec],
        scratch_shapes=scratch,
        compiler_params=_params(("arbitrary", "arbitrary"), vmem + (8 << 20)),
        name="conv_ffn_state" if has_state else "conv_ffn",
    )(*args)


def _rope_tables(pos, dim):
    half = dim // 2
    inv = ROPE_THETA ** (-jnp.arange(half, dtype=F32) / half)
    ang = pos.astype(F32)[:, None] * inv[None, :]
    cos = jnp.tile(jnp.concatenate([jnp.cos(ang), jnp.cos(ang)], axis=-1), (1, LANES // dim))
    sin = jnp.tile(jnp.concatenate([-jnp.sin(ang), jnp.sin(ang)], axis=-1), (1, LANES // dim))
    return cos, sin


def _pad_to(x, size, axis):
    pad = [(0, 0)] * x.ndim
    pad[axis] = (0, size - x.shape[axis])
    return jnp.pad(x, pad)


def _project(x2d, wts, tables, tm, rope_period):
    xn = _rms_call(x2d, wts["g_attn"], tm)
    w16 = wts["w_main"]
    (cos_h, sin_h), (cos_i, sin_i) = tables
    hw = H_A * HEAD_DIM
    kw = KV_B * HEAD_DIM
    call = functools.partial(_proj_call, xn, w16, tm=tm)
    (qa16,) = call(0, hw, tn=512, gain=jnp.tile(wts["g_qa"], H_A), want32=False, want16=True, name="proj_qa")
    ka32, ka16 = call(hw, hw, tn=512, gain=jnp.tile(wts["g_ka"], H_A), want32=True, want16=True, name="proj_ka")
    va32, va16 = call(2 * hw, hw, tn=512, want32=True, want16=True, name="proj_va")
    (qb16,) = call(3 * hw, hw, tn=512, gain=jnp.tile(wts["g_qb"], H_B), rope=(cos_h, sin_h, HEAD_DIM),
                   rope_period=rope_period, want32=False, want16=True, name="proj_qb")
    kb32, kb16 = call(4 * hw, kw, tn=kw, gain=jnp.tile(wts["g_kb"], KV_B), rope=(cos_h, sin_h, HEAD_DIM),
                      rope_period=rope_period, want32=True, want16=True, name="proj_kb")
    vb32, vb16 = call(4 * hw + kw, kw, tn=kw, want32=True, want16=True, name="proj_vb")
    (qi16,) = call(4 * hw + 2 * kw, H_IDX * D_IDX, tn=512, rope=(cos_i, sin_i, D_IDX),
                   rope_period=rope_period, want32=False, want16=True, name="proj_qi")
    ki32, ki16, logf, wi = _misc_call(xn, wts["w_misc"], wts["b_f"], cos_i, sin_i, tm, rope_period)
    return dict(qa16=qa16, ka32=ka32, ka16=ka16, va32=va32, va16=va16, qb16=qb16, kb32=kb32, kb16=kb16,
                vb32=vb32, vb16=vb16, qi16=qi16, ki32=ki32, ki16=ki16, logf=logf, wi=wi)


def _layer_weights(l, w_in, b_f, g_qa, g_ka, g_qb, g_kb, g_attn, w_out, g_ffn, w_gate, w_up, conv_w, conv_b, w_down):
    hw, kw = H_A * HEAD_DIM, KV_B * HEAD_DIM
    splits = (hw, hw, hw, H_A, H_B * HEAD_DIM, kw, kw, H_IDX * D_IDX, D_IDX, H_IDX)
    pts = [0] + [int(v) for v in np.cumsum(splits)]
    col = lambda k: w_in[l][:, pts[k]:pts[k + 1]]
    w_main = jnp.concatenate([col(0), col(1), col(2), col(4), col(5), col(6), col(7)], axis=1).astype(BF16)
    w_misc = jnp.concatenate([_pad_to(col(8), LANES, 1), _pad_to(col(3), LANES, 1), _pad_to(col(9), LANES, 1)],
                             axis=1).astype(BF16)
    dff = w_gate.shape[2]
    dffp = -(-dff // 512) * 512
    return dict(
        w_main=w_main, w_misc=w_misc, b_f=b_f[l], g_qa=g_qa[l], g_ka=g_ka[l], g_qb=g_qb[l], g_kb=g_kb[l],
        g_attn=g_attn[l], w_out=w_out[l].astype(BF16), g_ffn=g_ffn[l],
        w_gate=_pad_to(w_gate[l], dffp, 1).astype(BF16), w_up=_pad_to(w_up[l], dffp, 1).astype(BF16),
        conv_w=_pad_to(conv_w[l], dffp, 1), conv_b=_pad_to(conv_b[l], dffp, 0),
        w_down=_pad_to(w_down[l], dffp, 0).astype(BF16), dff=dff)


def _prompt_layer(x, wts, tables):
    nb, t, d = x.shape
    x2d = x.reshape(nb * t, d)
    pr = _project(x2d, wts, tables, tm=512, rope_period=t)
    logf_t = jnp.swapaxes(pr["logf"].reshape(nb, t, H_A), 1, 2)
    c_lanes = _cumsum_call(logf_t)
    c_rows = jnp.swapaxes(c_lanes, 1, 2).reshape(nb * t, H_A)
    oa = _fox_call(pr["qa16"], pr["ka16"], pr["va16"], c_rows, c_lanes, nb, t, tq=512)
    n_sel = min(TOPK_MAX, t // 4)
    sc, sc_t = _idx_call(pr["qi16"], pr["wi"], pr["ki16"], nb, t, tq=512)
    thr, jcut = [a.reshape(nb * t, 1) for a in _select_call(sc_t, n_sel, causal_steps=True)]
    ob = _dsa_call(pr["qb16"], pr["kb16"], pr["vb16"], sc, thr, jcut, nb, t, tq=512)
    h = _outproj_call(oa, ob, wts["w_out"], x2d, tm=512, tn=512)
    y, glast = _ffn_call(h, wts["g_ffn"], wts["w_gate"], wts["w_up"], wts["conv_w"], wts["conv_b"], wts["w_down"],
                         tm=512, tf=512, seq_rows=t)
    dff = wts["dff"]
    new_g = glast.reshape(nb, t // 512, HALO, -1)[:, -1, HALO - (CONV_W - 1):, :dff]
    rows = (pr["ka32"].reshape(nb, t, H_A, HEAD_DIM), pr["va32"].reshape(nb, t, H_A, HEAD_DIM),
            pr["logf"].reshape(nb, t, H_A), pr["kb32"].reshape(nb, t, KV_B, HEAD_DIM),
            pr["vb32"].reshape(nb, t, KV_B, HEAD_DIM), pr["ki32"].reshape(nb, t, D_IDX), new_g)
    return y.reshape(nb, t, d), rows


def _sample_layer(x, wts, tables, pools, state, page_table):
    nb, nt, d = x.shape
    assert nt == SUBLANES
    n = nb * nt
    x2d = x.reshape(n, d)
    pr = _project(x2d, wts, tables, tm=n, rope_period=n)
    fox_k, fox_v, fox_lf_t, dsa_k, dsa_v, idx_k = pools
    past_len = page_table.shape[1] * PAGE_SIZE

    def heads_first(a, heads, hd):
        return jnp.swapaxes(a.reshape(nb, nt, heads, hd), 1, 2)

    def head_rows(a, heads):
        return heads_first(a, heads, HEAD_DIM).reshape(nb, heads * nt, HEAD_DIM).astype(F32)

    def new_tile(a):
        return _pad_to(a.reshape(nb, nt, -1), PAGE_SIZE, 1)

    lfnew = _pad_to(jnp.swapaxes(pr["logf"].reshape(nb, nt, H_A), 1, 2), PAGE_SIZE, 2)
    oa = _fox_decode_call(page_table, head_rows(pr["qa16"], H_A), pr["logf"], new_tile(pr["ka16"]),
                          new_tile(pr["va16"]), lfnew, fox_k, fox_v, fox_lf_t, nt)
    n_sel = min(TOPK_MAX, (past_len + nt) // 4)
    qi_rows = heads_first(pr["qi16"], H_IDX, D_IDX).reshape(nb, H_IDX * nt, D_IDX)
    wi_rows = jnp.swapaxes(pr["wi"].reshape(nb, nt, H_IDX), 1, 2).reshape(nb, H_IDX * nt, 1)
    sc = _idx_decode_call(page_table, qi_rows, wi_rows, new_tile(pr["ki16"]), idx_k, nt)
    sc_t = jnp.transpose(sc, (2, 0, 1)).reshape(1, sc.shape[2], n)
    n_lanes = -(-n // SELECT_LANES) * SELECT_LANES
    thr, jcut = [a.reshape(n_lanes)[:n].reshape(nb, nt, 1)
                 for a in _select_call(_pad_to(sc_t, n_lanes, 2), n_sel, causal_steps=False)]
    ob = _dsa_decode_call(page_table, head_rows(pr["qb16"], H_B), new_tile(pr["kb16"]), new_tile(pr["vb16"]),
                          sc, thr, jcut, dsa_k, dsa_v, nt)
    h = _outproj_call(oa, ob, wts["w_out"], x2d, tm=n, tn=512)
    dffp = wts["w_gate"].shape[1]
    st = _pad_to(state, dffp, 2)
    zeros = jnp.zeros((nb, nt, dffp), F32)
    s1 = zeros.at[:, 0].set(st[:, 1]).reshape(n, dffp)
    s2 = zeros.at[:, 0].set(st[:, 0]).at[:, 1].set(st[:, 1]).reshape(n, dffp)
    y, gfull = _ffn_call(h, wts["g_ffn"], wts["w_gate"], wts["w_up"], wts["conv_w"], wts["conv_b"], wts["w_down"],
                         tm=n, tf=512, seq_rows=nt, state=(s1, s2))
    dff = wts["dff"]
    new_g = gfull.reshape(nb, nt, dffp)[:, nt - (CONV_W - 1):, :dff]
    rows = (pr["ka32"].reshape(nb, nt, H_A, HEAD_DIM), pr["va32"].reshape(nb, nt, H_A, HEAD_DIM),
            pr["logf"].reshape(nb, nt, H_A), pr["kb32"].reshape(nb, nt, KV_B, HEAD_DIM),
            pr["vb32"].reshape(nb, nt, KV_B, HEAD_DIM), pr["ki32"].reshape(nb, nt, D_IDX), new_g)
    return y.reshape(nb, nt, d), rows


def kernel(x_prompt, x_sample, cache_fox_k, cache_fox_v, cache_fox_logf, cache_dsa_k, cache_dsa_v, cache_idx_k,
           state_ffn_conv, page_table, w_in, b_f, g_qa, g_ka, g_qb, g_kb, g_attn, w_out, g_ffn, w_gate, w_up,
           conv_w, conv_b, w_down):
    depth = w_in.shape[0]
    t = x_prompt.shape[1]
    nb_s, nt_s = x_sample.shape[:2]
    past_len = page_table.shape[1] * PAGE_SIZE
    pos_p = jnp.arange(t, dtype=I32)
    pos_s = jnp.tile(past_len + jnp.arange(nt_s, dtype=I32), nb_s)
    tables_p = (_rope_tables(pos_p, HEAD_DIM), _rope_tables(pos_p, D_IDX))
    tables_s = (_rope_tables(pos_s, HEAD_DIM), _rope_tables(pos_s, D_IDX))
    n_pool = cache_fox_k.shape[1]
    head_pages = lambda pool: pool.reshape(depth * n_pool, PAGE_SIZE * pool.shape[3], pool.shape[4])
    flat = lambda pool: pool.reshape(depth * n_pool, PAGE_SIZE, -1)
    pools = (head_pages(cache_fox_k), head_pages(cache_fox_v), jnp.swapaxes(flat(cache_fox_logf), 1, 2),
             head_pages(cache_dsa_k), head_pages(cache_dsa_v), flat(cache_idx_k))
    y_p, y_s = x_prompt, x_sample
    rows_p, rows_s = [], []
    for l in range(depth):
        wts = _layer_weights(l, w_in, b_f, g_qa, g_ka, g_qb, g_kb, g_attn, w_out, g_ffn, w_gate, w_up,
                             conv_w, conv_b, w_down)
        y_p, new_p = _prompt_layer(y_p, wts, tables_p)
        y_s, new_s = _sample_layer(y_s, wts, tables_s, pools, state_ffn_conv[l], page_table + l * n_pool)
        rows_p.append(new_p)
        rows_s.append(new_s)
    stack = lambda vs: vs[0][None] if len(vs) == 1 else jnp.stack(vs)
    out_p = [stack(v) for v in zip(*rows_p)]
    out_s = [stack(v) for v in zip(*rows_s)]
    return (y_p, y_s, *out_p, *out_s)
```
